```python
import jax, jax.numpy as jnp
from jax import lax
import numpy as np

D_MODEL = 2048
BATCH = 2
SEQ = 8192
DEPTH = 4

GRID_W = 64
CTX_LEN = 256
N_MIXERS = 3
NORM_EPS = 1e-6
ROPE_THETA = 10000.0
NEG_INF = -1e30
ADA_CHUNKS = 6

MLA_HEADS = 16
MLA_Q_RANK = 512
MLA_KV_RANK = 512
MLA_NOPE = 128
MLA_ROPE = 64
MLA_V = 128
Q_BLOCK = 128

SSM_D_INNER = 2 * D_MODEL
SSM_HEADDIM = 64
SSM_HEADS = SSM_D_INNER // SSM_HEADDIM
SSM_GROUPS = 8
SSM_STATE = 128
SSM_CONV = 5
SSM_CHUNK = 128
SSM_CONV_DIM = SSM_D_INNER + 2 * SSM_GROUPS * SSM_STATE
SSM_IN_DIM = SSM_D_INNER + SSM_CONV_DIM + 2 * SSM_HEADS

WINDOW = 128
W_BLOCK = WINDOW
WQ_HEADS = 32
WKV_HEADS = 8
W_HEAD_DIM = 64

PEER_HEADS = 8
PEER_N_KEYS = 128
PEER_N_EXPERTS = PEER_N_KEYS * PEER_N_KEYS
PEER_TOPK = 16
PEER_KEY_DIM = 256
PEER_TOKEN_BLOCK = 64

kernel_name = 'hybrid_mla_ssd_swa_peer_dit'


def rms_norm(x, g):
    xf = x.astype(jnp.float32)
    y = xf * lax.rsqrt(jnp.mean(xf * xf, axis=-1, keepdims=True) + NORM_EPS)
    return (y * g.astype(jnp.float32)).astype(x.dtype)


def adaln(cvec, w, b):
    m = jax.nn.silu(cvec) @ w + b
    return jnp.split(m[:, None, :], ADA_CHUNKS, axis=-1)


def modulate(h, shift, scale):
    return h * (1 + scale) + shift


def grid_positions(n_tokens):
    n_rows = n_tokens // GRID_W
    rows = jnp.repeat(jnp.arange(n_rows, dtype=jnp.int32), GRID_W)
    cols = jnp.tile(jnp.arange(GRID_W, dtype=jnp.int32), n_rows)
    return rows, cols


def rope_1d(x, pos):
    half = x.shape[-1] // 2
    freqs = ROPE_THETA ** (-jnp.arange(half, dtype=jnp.float32) / half)
    ang = pos.astype(jnp.float32)[:, None] * freqs
    cos = jnp.cos(ang)[:, None, :].astype(x.dtype)
    sin = jnp.sin(ang)[:, None, :].astype(x.dtype)
    x1, x2 = x[..., :half], x[..., half:]
    return jnp.concatenate([x1 * cos - x2 * sin, x2 * cos + x1 * sin], axis=-1)


def rope_2d(x, rows, cols):
    half = x.shape[-1] // 2
    return jnp.concatenate([rope_1d(x[..., :half], rows), rope_1d(x[..., half:], cols)], axis=-1)


def softmax_attend(q, k, v, scale, mask=None, sink=None):
    B_, Lq, H, d = q.shape
    KH = k.shape[2]
    G = H // KH
    qg = q.reshape(B_, Lq, KH, G, d)
    s = jnp.einsum('bqkgd,bskd->bkgqs', qg, k).astype(jnp.float32) * scale
    if mask is not None:
        s = jnp.where(mask, s, NEG_INF)
    if sink is not None:
        sk = jnp.broadcast_to(sink.astype(jnp.float32).reshape(1, KH, G, 1, 1), s.shape[:-1] + (1,))
        s = jnp.concatenate([s, sk], axis=-1)
    p = jax.nn.softmax(s, axis=-1)
    if sink is not None:
        p = p[..., :-1]
    o = jnp.einsum('bkgqs,bskd->bqkgd', p.astype(v.dtype), v)
    return o.reshape(B_, Lq, H, v.shape[-1])


def sweep_query_blocks(q, k, v, scale):
    B_, L, H, d = q.shape
    nb = L // Q_BLOCK
    qb = jnp.moveaxis(q.reshape(B_, nb, Q_BLOCK, H, d), 1, 0)
    o = lax.map(lambda t: softmax_attend(t, k, v, scale), qb)
    return jnp.moveaxis(o, 0, 1).reshape(B_, L, H, v.shape[-1])


def mla_mixer(h_lat, h_ctx, w_a, q_g, kv_g, w_uq, w_ukv, w_o, rows, cols, need_ctx):
    scale = (MLA_NOPE + MLA_ROPE) ** -0.5

    def compress(h):
        return jnp.split(h @ w_a, [MLA_Q_RANK, MLA_Q_RANK + MLA_KV_RANK], axis=-1)

    def queries(c_q, rotate):
        B_, L, _ = c_q.shape
        q = (rms_norm(c_q, q_g) @ w_uq).reshape(B_, L, MLA_HEADS, MLA_NOPE + MLA_ROPE)
        q_nope, q_rope = jnp.split(q, [MLA_NOPE], axis=-1)
        if rotate:
            q_rope = rope_2d(q_rope, rows, cols)
        return jnp.concatenate([q_nope, q_rope], axis=-1)

    def keys_values(c_kv, k_r, rotate):
        B_, L, _ = c_kv.shape
        kv = (rms_norm(c_kv, kv_g) @ w_ukv).reshape(B_, L, MLA_HEADS, MLA_NOPE + MLA_V)
        k_nope, v = jnp.split(kv, [MLA_NOPE], axis=-1)
        k_rope = k_r[:, :, None, :]
        if rotate:
            k_rope = rope_2d(k_rope, rows, cols)
        k = jnp.concatenate([k_nope, jnp.broadcast_to(k_rope, (B_, L, MLA_HEADS, MLA_ROPE))], axis=-1)
        return k, v

    cq_l, ckv_l, kr_l = compress(h_lat)
    cq_c, ckv_c, kr_c = compress(h_ctx)
    q_l = queries(cq_l, True)
    k_l, v_l = keys_values(ckv_l, kr_l, True)
    k_c, v_c = keys_values(ckv_c, kr_c, False)
    k_all = jnp.concatenate([k_l, k_c], axis=1)
    v_all = jnp.concatenate([v_l, v_c], axis=1)
    o_l = sweep_query_blocks(q_l, k_all, v_all, scale)
    B_, L = h_lat.shape[:2]
    out_l = o_l.reshape(B_, L, MLA_HEADS * MLA_V) @ w_o
    out_c = None
    if need_ctx:
        o_c = softmax_attend(queries(cq_c, False), k_c, v_c, scale)
        out_c = o_c.reshape(B_, h_ctx.shape[1], MLA_HEADS * MLA_V) @ w_o
    return out_l, out_c


def depthwise_conv_centred(x, w, b):
    ch = x.shape[-1]
    pad = (w.shape[0] - 1) // 2
    y = lax.conv_general_dilated(x, w[:, None, :].astype(x.dtype), (1,), [(pad, pad)],
                                 dimension_numbers=('NWC', 'WIO', 'NWC'), feature_group_count=ch)
    return y + b


def ssd_scan(xs, dt, A, bm, cm, h0):
    B_, L, H, P = xs.shape
    G = bm.shape[2]
    HG = H // G
    Q = SSM_CHUNK
    nc = L // Q

    def chunks(t):
        return jnp.moveaxis(t.reshape((B_, nc, Q) + t.shape[2:]), 1, 0)

    x_c = chunks(xs.reshape(B_, L, G, HG, P))
    dt_c = chunks(dt.reshape(B_, L, G, HG))
    b_c = chunks(bm)
    c_c = chunks(cm)
    A_g = A.reshape(G, HG)
    tri = jnp.tril(jnp.ones((Q, Q), dtype=bool))[None, :, :, None, None]

    def step(h, inp):
        xq, dtq, bq, cq = inp
        acs = jnp.cumsum(dtq * A_g, axis=1)
        seg = acs[:, :, None] - acs[:, None, :]
        decay = jnp.exp(jnp.where(tri, seg, -jnp.inf))
        cb = jnp.einsum('bign,bjgn->bijg', cq, bq)
        y_diag = jnp.einsum('bijg,bijgh,bjgh,bjghp->bighp', cb, decay, dtq, xq)
        y_off = jnp.einsum('bign,bghpn,bigh->bighp', cq, h, jnp.exp(acs))
        last = acs[:, -1]
        w_state = jnp.exp(last[:, None] - acs) * dtq
        h_new = jnp.exp(last)[..., None, None] * h + jnp.einsum('bjgn,bjgh,bjghp->bghpn', bq, w_state, xq)
        return h_new, y_diag + y_off

    h_fin, y = lax.scan(step, h0, (x_c, dt_c, b_c, c_c))
    y = jnp.moveaxis(y, 0, 1).reshape(B_, L, H, P)
    return y, h_fin


def ssd_inputs(h, w_in, conv_w, conv_b, dt_bias):
    B_, L, _ = h.shape
    z, xbc, dt_raw = jnp.split(h @ w_in, [SSM_D_INNER, SSM_D_INNER + SSM_CONV_DIM], axis=-1)
    xbc = jax.nn.silu(depthwise_conv_centred(xbc, conv_w, conv_b))
    xs, bm, cm = jnp.split(xbc, [SSM_D_INNER, SSM_D_INNER + SSM_GROUPS * SSM_STATE], axis=-1)
    xs = xs.reshape(B_, L, SSM_HEADS, SSM_HEADDIM)
    bm = bm.reshape(B_, L, SSM_GROUPS, SSM_STATE)
    cm = cm.reshape(B_, L, SSM_GROUPS, SSM_STATE)
    dt = jax.nn.softplus(dt_raw.astype(jnp.float32).reshape(B_, L, 2, SSM_HEADS) + dt_bias.astype(jnp.float32))
    return z, xs, bm, cm, dt


def bidirectional_ssd(xs, bm, cm, dt, A, init_f, init_b):
    flip = lambda t: jnp.flip(t, axis=1)
    y_f, s_f = ssd_scan(xs, dt[:, :, 0], A[0], bm, cm, init_f)
    y_b, s_b = ssd_scan(flip(xs), flip(dt[:, :, 1]), A[1], flip(bm), flip(cm), init_b)
    return y_f + flip(y_b), s_f, s_b


def ssd_mixer(h_lat, h_ctx, w_in, conv_w, conv_b, a_log, dt_bias, d_skip, norm_g, w_out, need_ctx):
    A = -jnp.exp(a_log.astype(jnp.float32))
    z_c, x_c, b_c, c_c, dt_c = ssd_inputs(h_ctx, w_in, conv_w, conv_b, dt_bias)
    z_l, x_l, b_l, c_l, dt_l = ssd_inputs(h_lat, w_in, conv_w, conv_b, dt_bias)
    B_ = h_lat.shape[0]
    h0 = jnp.zeros((B_, SSM_GROUPS, SSM_HEADS // SSM_GROUPS, SSM_HEADDIM, SSM_STATE), jnp.float32)
    y_c, s_f, s_b = bidirectional_ssd(x_c, b_c, c_c, dt_c, A, h0, h0)
    y_l, _, _ = bidirectional_ssd(x_l, b_l, c_l, dt_l, A, s_f, s_b)

    def finish(y, xs, z):
        y = y + d_skip.astype(jnp.float32)[:, None] * xs
        y = y.reshape(z.shape[0], z.shape[1], SSM_D_INNER).astype(z.dtype)
        return rms_norm(y * jax.nn.silu(z), norm_g) @ w_out

    out_l = finish(y_l, x_l, z_l)
    out_c = finish(y_c, x_c, z_c) if need_ctx else None
    return out_l, out_c


def banded_window_attention(q, k, v, k_ctx, v_ctx, sinks, scale):
    B_, L, H, d = q.shape
    nb = L // W_BLOCK
    pad = ((0, 0), (W_BLOCK, W_BLOCK), (0, 0), (0, 0))
    kp = jnp.pad(k, pad)
    vp = jnp.pad(v, pad)
    q_off = jnp.arange(W_BLOCK)
    k_off = jnp.arange(3 * W_BLOCK)
    ctx_mask = jnp.ones((W_BLOCK, k_ctx.shape[1]), dtype=bool)

    def one_block(i):
        start = i * W_BLOCK
        qb = lax.dynamic_slice_in_dim(q, start, W_BLOCK, axis=1)
        kb = lax.dynamic_slice_in_dim(kp, start, 3 * W_BLOCK, axis=1)
        vb = lax.dynamic_slice_in_dim(vp, start, 3 * W_BLOCK, axis=1)
        qpos = start + q_off
        kpos = start - W_BLOCK + k_off
        band = (jnp.abs(qpos[:, None] - kpos[None, :]) <= WINDOW) & ((kpos >= 0) & (kpos < L))[None, :]
        mask = jnp.concatenate([band, ctx_mask], axis=1)
        return softmax_attend(qb, jnp.concatenate([kb, k_ctx], axis=1), jnp.concatenate([vb, v_ctx], axis=1),
                              scale, mask=mask, sink=sinks)

    o = lax.map(one_block, jnp.arange(nb))
    return jnp.moveaxis(o, 0, 1).reshape(B_, L, H, v.shape[-1])


def window_mixer(h_lat, h_ctx, w_qkv, sinks, w_o, rows, cols, need_ctx):
    scale = W_HEAD_DIM ** -0.5

    def project(h):
        B_, L, _ = h.shape
        q, k, v = jnp.split(h @ w_qkv, [WQ_HEADS * W_HEAD_DIM, (WQ_HEADS + WKV_HEADS) * W_HEAD_DIM], axis=-1)
        return (q.reshape(B_, L, WQ_HEADS, W_HEAD_DIM), k.reshape(B_, L, WKV_HEADS, W_HEAD_DIM),
                v.reshape(B_, L, WKV_HEADS, W_HEAD_DIM))

    q_l, k_l, v_l = project(h_lat)
    q_l = rope_2d(q_l, rows, cols)
    k_l = rope_2d(k_l, rows, cols)
    q_c, k_c, v_c = project(h_ctx)
    o_l = banded_window_attention(q_l, k_l, v_l, k_c, v_c, sinks, scale)
    B_, L = h_lat.shape[:2]
    out_l = o_l.reshape(B_, L, WQ_HEADS * W_HEAD_DIM) @ w_o
    out_c = None
    if need_ctx:
        o_c = softmax_attend(q_c, k_c, v_c, scale, sink=sinks)
        out_c = o_c.reshape(B_, h_ctx.shape[1], WQ_HEADS * W_HEAD_DIM) @ w_o
    return out_l, out_c


def peer_ffn(h, w_q, k1, k2, u, v):
    B_, L, D = h.shape
    T = B_ * L
    hf = h.reshape(T, D)
    q = (hf @ w_q).reshape(T, PEER_HEADS, 2, PEER_KEY_DIM // 2)
    s1 = jnp.einsum('thd,hkd->thk', q[:, :, 0], k1).astype(jnp.float32)
    s2 = jnp.einsum('thd,hkd->thk', q[:, :, 1], k2).astype(jnp.float32)
    v1, i1 = lax.top_k(s1, PEER_TOPK)
    v2, i2 = lax.top_k(s2, PEER_TOPK)
    n_cand = PEER_TOPK * PEER_TOPK
    cand_s = (v1[..., :, None] + v2[..., None, :]).reshape(T, PEER_HEADS, n_cand)
    cand_i = (i1[..., :, None] * PEER_N_KEYS + i2[..., None, :]).reshape(T, PEER_HEADS, n_cand)
    top_s, pos = lax.top_k(cand_s, PEER_TOPK)
    experts = jnp.take_along_axis(cand_i, pos, axis=-1)
    gates = jax.nn.softmax(top_s, axis=-1).astype(h.dtype)
    nb = T // PEER_TOKEN_BLOCK

    def block(args):
        hb, eb, gb = args
        act = jax.nn.gelu(jnp.einsum('thkd,td->thk', u[eb], hb))
        return jnp.einsum('thk,thkd->td', gb * act, v[eb])

    out = lax.map(block, (hf.reshape(nb, PEER_TOKEN_BLOCK, D),
                          experts.reshape(nb, PEER_TOKEN_BLOCK, PEER_HEADS, PEER_TOPK),
                          gates.reshape(nb, PEER_TOKEN_BLOCK, PEER_HEADS, PEER_TOPK)))
    return out.reshape(B_, L, D)


def setup_inputs(seed: int = 0) -> dict:
    key = jax.random.key(seed)
    ks = iter(jax.random.split(key, 64))
    n_mla = (DEPTH + 2) // N_MIXERS
    n_ssd = (DEPTH + 1) // N_MIXERS
    n_win = DEPTH // N_MIXERS
    f32 = jnp.float32

    def nrm(shape, fan_in, g=1.0):
        return jax.random.normal(next(ks), shape, f32) * (g * fan_in ** -0.5)

    def gain(shape):
        return 1.0 + 0.05 * jax.random.normal(next(ks), shape, f32)

    def small(shape, s):
        return s * jax.random.normal(next(ks), shape, f32)

    dt0 = jnp.exp(jax.random.uniform(next(ks), (n_ssd, 2, SSM_HEADS), f32, jnp.log(1e-3), jnp.log(1e-1)))
    return {
        'x': jax.random.normal(next(ks), (BATCH, SEQ, D_MODEL), f32),
        'c': jax.random.normal(next(ks), (BATCH, D_MODEL), f32),
        'ctx': jax.random.normal(next(ks), (BATCH, CTX_LEN, D_MODEL), f32),
        'c_ctx': jax.random.normal(next(ks), (D_MODEL,), f32),
        'ada_w': nrm((DEPTH, D_MODEL, ADA_CHUNKS * D_MODEL), D_MODEL, 0.5),
        'ada_b': small((DEPTH, ADA_CHUNKS * D_MODEL), 0.02),
        'norm_mix_g': gain((DEPTH, D_MODEL)),
        'norm_ffn_g': gain((DEPTH, D_MODEL)),
        'mla_w_a': nrm((n_mla, D_MODEL, MLA_Q_RANK + MLA_KV_RANK + MLA_ROPE), D_MODEL),
        'mla_q_norm_g': gain((n_mla, MLA_Q_RANK)),
        'mla_kv_norm_g': gain((n_mla, MLA_KV_RANK)),
        'mla_w_uq': nrm((n_mla, MLA_Q_RANK, MLA_HEADS * (MLA_NOPE + MLA_ROPE)), MLA_Q_RANK),
        'mla_w_ukv': nrm((n_mla, MLA_KV_RANK, MLA_HEADS * (MLA_NOPE + MLA_V)), MLA_KV_RANK),
        'mla_w_o': nrm((n_mla, MLA_HEADS * MLA_V, D_MODEL), MLA_HEADS * MLA_V),
        'ssd_w_in': nrm((n_ssd, D_MODEL, SSM_IN_DIM), D_MODEL),
        'ssd_conv_w': nrm((n_ssd, SSM_CONV, SSM_CONV_DIM), SSM_CONV),
        'ssd_conv_b': small((n_ssd, SSM_CONV_DIM), 0.02),
        'ssd_a_log': jnp.log(jax.random.uniform(next(ks), (n_ssd, 2, SSM_HEADS), f32, 1.0, 16.0)),
        'ssd_dt_bias': dt0 + jnp.log(-jnp.expm1(-dt0)),
        'ssd_d_skip': gain((n_ssd, SSM_HEADS)),
        'ssd_norm_g': gain((n_ssd, SSM_D_INNER)),
        'ssd_w_out': nrm((n_ssd, SSM_D_INNER, D_MODEL), SSM_D_INNER),
        'win_w_qkv': nrm((n_win, D_MODEL, (WQ_HEADS + 2 * WKV_HEADS) * W_HEAD_DIM), D_MODEL),
        'win_sinks': small((n_win, WQ_HEADS), 0.5),
        'win_w_o': nrm((n_win, WQ_HEADS * W_HEAD_DIM, D_MODEL), WQ_HEADS * W_HEAD_DIM),
        'peer_w_q': nrm((DEPTH, D_MODEL, PEER_HEADS * PEER_KEY_DIM), D_MODEL),
        'peer_k1': nrm((DEPTH, PEER_HEADS, PEER_N_KEYS, PEER_KEY_DIM // 2), PEER_KEY_DIM // 2),
        'peer_k2': nrm((DEPTH, PEER_HEADS, PEER_N_KEYS, PEER_KEY_DIM // 2), PEER_KEY_DIM // 2),
        'peer_u': nrm((DEPTH, PEER_N_EXPERTS, D_MODEL), D_MODEL),
        'peer_v': nrm((DEPTH, PEER_N_EXPERTS, D_MODEL), PEER_HEADS),
        'final_norm_g': gain((D_MODEL,)),
    }


def reference(x, c, ctx, c_ctx, ada_w, ada_b, norm_mix_g, norm_ffn_g, mla_w_a, mla_q_norm_g, mla_kv_norm_g,
              mla_w_uq, mla_w_ukv, mla_w_o, ssd_w_in, ssd_conv_w, ssd_conv_b, ssd_a_log, ssd_dt_bias, ssd_d_skip,
              ssd_norm_g, ssd_w_out, win_w_qkv, win_sinks, win_w_o, peer_w_q, peer_k1, peer_k2, peer_u, peer_v,
              final_norm_g):
    rows, cols = grid_positions(x.shape[1])
    for i in range(DEPTH):
        need_ctx = i < DEPTH - 1
        sh1, sc1, g1, sh2, sc2, g2 = adaln(c, ada_w[i], ada_b[i])
        csh1, csc1, cg1, csh2, csc2, cg2 = adaln(c_ctx[None, :], ada_w[i], ada_b[i])
        h_l = modulate(rms_norm(x, norm_mix_g[i]), sh1, sc1)
        h_c = modulate(rms_norm(ctx, norm_mix_g[i]), csh1, csc1)
        kind, j = i % N_MIXERS, i // N_MIXERS
        if kind == 0:
            mix_l, mix_c = mla_mixer(h_l, h_c, mla_w_a[j], mla_q_norm_g[j], mla_kv_norm_g[j], mla_w_uq[j],
                                     mla_w_ukv[j], mla_w_o[j], rows, cols, need_ctx)
        elif kind == 1:
            mix_l, mix_c = ssd_mixer(h_l, h_c, ssd_w_in[j], ssd_conv_w[j], ssd_conv_b[j], ssd_a_log[j],
                                     ssd_dt_bias[j], ssd_d_skip[j], ssd_norm_g[j], ssd_w_out[j], need_ctx)
        else:
            mix_l, mix_c = window_mixer(h_l, h_c, win_w_qkv[j], win_sinks[j], win_w_o[j], rows, cols, need_ctx)
        x = x + g1 * mix_l
        x = x + g2 * peer_ffn(modulate(rms_norm(x, norm_ffn_g[i]), sh2, sc2),
                              peer_w_q[i], peer_k1[i], peer_k2[i], peer_u[i], peer_v[i])
        if need_ctx:
            ctx = ctx + cg1 * mix_c
            ctx = ctx + cg2 * peer_ffn(modulate(rms_norm(ctx, norm_ffn_g[i]), csh2, csc2),
                                       peer_w_q[i], peer_k1[i], peer_k2[i], peer_u[i], peer_v[i])
    return rms_norm(x, final_norm_g)
```

```python
import functools
import math

import jax
import jax.numpy as jnp
from jax import lax
from jax.experimental import pallas as pl
from jax.experimental.pallas import tpu as pltpu

F32 = jnp.float32
BF16 = jnp.bfloat16

D_MODEL = 2048
DEPTH = 4
GRID_W = 64
N_MIXERS = 3
NORM_EPS = 1e-6
ROPE_THETA = 10000.0
NEG_INF = -1e30
ADA_CHUNKS = 6

MLA_HEADS = 16
MLA_Q_RANK = 512
MLA_KV_RANK = 512
MLA_NOPE = 128
MLA_ROPE = 64
MLA_V = 128

SSM_D_INNER = 2 * D_MODEL
SSM_HEADDIM = 64
SSM_HEADS = SSM_D_INNER // SSM_HEADDIM
SSM_GROUPS = 8
SSM_HG = SSM_HEADS // SSM_GROUPS
SSM_STATE = 128
SSM_CONV = 5
SSM_CHUNK = 128
SSM_BC = SSM_GROUPS * SSM_STATE
SSM_CONV_DIM = SSM_D_INNER + 2 * SSM_BC

WINDOW = 128
WQ_HEADS = 32
WKV_HEADS = 8
W_HEAD_DIM = 64

PEER_HEADS = 8
PEER_N_KEYS = 128
PEER_N_EXPERTS = PEER_N_KEYS * PEER_N_KEYS
PEER_TOPK = 16
PEER_HALF = 128

LANES = 128
VMEM_LIMIT = 56 * 1024 * 1024
ROW_TILE = 256


def _pick(n, candidates):
    for c in candidates:
        if n % c == 0:
            return c
    raise ValueError(f"no tile for {n} in {candidates}")


def _cparams(sem):
    return pltpu.CompilerParams(dimension_semantics=sem, vmem_limit_bytes=VMEM_LIMIT)


def _mm_kernel(a_ref, w_ref, o_ref):
    o_ref[...] = jnp.dot(a_ref[...], w_ref[...], preferred_element_type=F32).astype(o_ref.dtype)


def matmul(a, w, out_dtype, name):
    M, K = a.shape
    N = w.shape[1]
    tm = _pick(M, (1056, 1024, 768, 512, 384, 256, 128, 64, 32, 16))
    tn = _pick(N, (512, 384, 256, 128))
    return pl.pallas_call(
        _mm_kernel,
        grid=(M // tm, N // tn),
        in_specs=[pl.BlockSpec((tm, K), lambda i, j: (i, 0)),
                  pl.BlockSpec((K, tn), lambda i, j: (0, j))],
        out_specs=pl.BlockSpec((tm, tn), lambda i, j: (i, j)),
        out_shape=jax.ShapeDtypeStruct((M, N), out_dtype),
        compiler_params=_cparams(("parallel", "parallel")),
        name=name,
    )(a, w)


def _adaln_kernel(c_ref, w_ref, b_ref, o_ref):
    c = c_ref[...]
    s = c * (1.0 / (1.0 + jnp.exp(-c)))
    o_ref[...] = jnp.dot(s, w_ref[...], preferred_element_type=F32,
                         precision=lax.Precision.HIGHEST) + b_ref[...]


def adaln_all(cmat, ada_w, ada_b):
    depth, d, n = ada_w.shape
    rows = cmat.shape[0]
    tn = _pick(n, (1024, 512, 256, 128))
    return pl.pallas_call(
        _adaln_kernel,
        grid=(depth, n // tn),
        in_specs=[pl.BlockSpec((rows, d), lambda l, j: (0, 0)),
                  pl.BlockSpec((None, d, tn), lambda l, j: (l, 0, j)),
                  pl.BlockSpec((None, 1, tn), lambda l, j: (l, 0, j))],
        out_specs=pl.BlockSpec((None, rows, tn), lambda l, j: (l, 0, j)),
        out_shape=jax.ShapeDtypeStruct((depth, rows, n), F32),
        compiler_params=_cparams(("parallel", "parallel")),
        name="adaln",
    )(cmat, ada_w, ada_b.reshape(depth, 1, n))


def _rms(x, g):
    var = jnp.mean(x * x, axis=-1, keepdims=True)
    return x * lax.rsqrt(var + NORM_EPS) * g


def _rnm_kernel(*refs, has_mix, modulate):
    it = iter(refs)
    x_ref = next(it)
    if has_mix:
        mix_ref, gate_ref = next(it), next(it)
    g_ref = next(it)
    if modulate:
        shift_ref, scale_ref = next(it), next(it)
    if has_mix and modulate:
        xo_ref = next(it)
    h_ref = next(it)
    x = x_ref[...]
    if has_mix:
        x = x + gate_ref[...] * mix_ref[...].astype(F32)
        if modulate:
            xo_ref[...] = x
    y = _rms(x, g_ref[...])
    if modulate:
        y = y * (1.0 + scale_ref[...]) + shift_ref[...]
    h_ref[...] = y.astype(h_ref.dtype)


def resid_norm_mod(x, mix, gate_mod, gate_k, norm_g, mod, shift_k, scale_k, n_lat, *, final=False):
    B, S, D = x.shape
    tr = ROW_TILE
    lat_tiles = n_lat // tr
    tiles = lat_tiles if final else S // tr
    has_mix = mix is not None
    modulate = not final

    def grp(b, t):
        return jnp.where(t < lat_tiles, b, B)

    row = pl.BlockSpec((None, tr, D), lambda b, t: (b, t, 0))

    def modspec(k):
        return pl.BlockSpec((None, None, 1, D), lambda b, t: (grp(b, t), k, 0, 0))

    args, specs = [x], [row]
    if has_mix:
        args += [mix, gate_mod]
        specs += [row, modspec(gate_k)]
    args.append(norm_g.reshape(1, D))
    specs.append(pl.BlockSpec((1, D), lambda b, t: (0, 0)))
    if modulate:
        args += [mod, mod]
        specs += [modspec(shift_k), modspec(scale_k)]
    out_shapes, out_specs = [], []
    if has_mix and modulate:
        out_shapes.append(jax.ShapeDtypeStruct((B, S, D), F32))
        out_specs.append(row)
    if final:
        out_shapes.append(jax.ShapeDtypeStruct((B, n_lat, D), F32))
    else:
        out_shapes.append(jax.ShapeDtypeStruct((B, S, D), BF16))
    out_specs.append(row)
    outs = pl.pallas_call(
        functools.partial(_rnm_kernel, has_mix=has_mix, modulate=modulate),
        grid=(B, tiles),
        in_specs=specs,
        out_specs=out_specs,
        out_shape=out_shapes,
        compiler_params=_cparams(("parallel", "parallel")),
        name="resid_norm_mod",
    )(*args)
    return outs


def rope_tables(n_lat, n_ctx, repeat_upper):
    half = MLA_ROPE // 4
    pos = jnp.arange(n_lat, dtype=jnp.int32)
    rows = (pos // GRID_W).astype(F32)
    cols = (pos % GRID_W).astype(F32)
    freqs = ROPE_THETA ** (-jnp.arange(half, dtype=F32) / half)
    ang_r = rows[:, None] * freqs
    ang_c = cols[:, None] * freqs
    ang = jnp.concatenate([ang_r, ang_r, ang_c, ang_c], axis=-1)
    cos, sin = jnp.cos(ang), jnp.sin(ang)
    first = (jnp.arange(64) % 32) < 16
    sa = jnp.where(first, -sin, 0.0)
    sb = jnp.where(first, 0.0, sin)
    cos = jnp.concatenate([cos, jnp.ones((n_ctx, 64), F32)], axis=0)
    sa = jnp.concatenate([sa, jnp.zeros((n_ctx, 64), F32)], axis=0)
    sb = jnp.concatenate([sb, jnp.zeros((n_ctx, 64), F32)], axis=0)
    if repeat_upper:
        return tuple(jnp.concatenate([t, t], axis=-1) for t in (cos, sa, sb))
    z = jnp.zeros_like(cos)
    return tuple(jnp.concatenate([t, z], axis=-1) for t in (cos, sa, sb))


def _rope128(x, cos, sa, sb):
    return x * cos + pltpu.roll(x, 112, 1) * sa + pltpu.roll(x, 16, 1) * sb


def _rope_cols_kernel(x_ref, cos_ref, sa_ref, sb_ref, o_ref):
    cos, sa, sb = cos_ref[...], sa_ref[...], sb_ref[...]
    for c in range(x_ref.shape[-1] // LANES):
        sl = slice(c * LANES, (c + 1) * LANES)
        o_ref[:, sl] = _rope128(x_ref[:, sl].astype(F32), cos, sa, sb).astype(o_ref.dtype)


def rope_cols(x, width, tables):
    B, S, _ = x.shape
    tr = ROW_TILE
    tab = pl.BlockSpec((tr, LANES), lambda b, t: (t, 0))
    return pl.pallas_call(
        _rope_cols_kernel,
        grid=(B, S // tr),
        in_specs=[pl.BlockSpec((None, tr, width), lambda b, t: (b, t, 0)), tab, tab, tab],
        out_specs=pl.BlockSpec((None, tr, width), lambda b, t: (b, t, 0)),
        out_shape=jax.ShapeDtypeStruct((B, S, width), BF16),
        compiler_params=_cparams(("parallel", "parallel")),
        name="rope_cols",
    )(x, *tables)


def _mla_prep_kernel(ca_ref, qg_ref, kvg_ref, cos_ref, sa_ref, sb_ref, qc_ref, kvc_ref, kr_ref):
    cq = ca_ref[:, :MLA_Q_RANK]
    ckv = ca_ref[:, MLA_Q_RANK:MLA_Q_RANK + MLA_KV_RANK]
    kr = ca_ref[:, MLA_Q_RANK + MLA_KV_RANK:]
    qc_ref[...] = _rms(cq, qg_ref[...]).astype(BF16)
    kvc_ref[...] = _rms(ckv, kvg_ref[...]).astype(BF16)
    kr_ref[...] = _rope128(kr, cos_ref[...], sa_ref[...], sb_ref[...]).astype(BF16)


def mla_prep(ca, q_g, kv_g, tables):
    B, S, C = ca.shape
    tr = ROW_TILE
    tab = pl.BlockSpec((tr, LANES), lambda b, t: (t, 0))

    def row(w):
        return pl.BlockSpec((None, tr, w), lambda b, t: (b, t, 0))

    def vec(w):
        return pl.BlockSpec((1, w), lambda b, t: (0, 0))

    return pl.pallas_call(
        _mla_prep_kernel,
        grid=(B, S // tr),
        in_specs=[row(C), vec(MLA_Q_RANK), vec(MLA_KV_RANK), tab, tab, tab],
        out_specs=[row(MLA_Q_RANK), row(MLA_KV_RANK), row(LANES)],
        out_shape=[jax.ShapeDtypeStruct((B, S, MLA_Q_RANK), BF16),
                   jax.ShapeDtypeStruct((B, S, MLA_KV_RANK), BF16),
                   jax.ShapeDtypeStruct((B, S, LANES), BF16)],
        compiler_params=_cparams(("parallel", "parallel")),
        name="mla_prep",
    )(ca, q_g.reshape(1, -1), kv_g.reshape(1, -1), *tables)


def _mla_attn_kernel(qn_ref, qr_ref, cos_ref, sa_ref, sb_ref, kn_ref, kr_ref, v_ref, o_ref,
                     q_sc, m_sc, l_sc, acc_sc, *, scale, nk):
    j = pl.program_id(3)

    @pl.when(j == 0)
    def _():
        qr = _rope128(qr_ref[...].astype(F32), cos_ref[...], sa_ref[...], sb_ref[...])
        q_sc[:, :LANES] = qn_ref[...]
        q_sc[:, LANES:] = qr.astype(BF16)
        m_sc[...] = jnp.full(m_sc.shape, -jnp.inf, F32)
        l_sc[...] = jnp.zeros(l_sc.shape, F32)
        acc_sc[...] = jnp.zeros(acc_sc.shape, F32)

    k = jnp.concatenate([kn_ref[...], kr_ref[...]], axis=1)
    s = lax.dot_general(q_sc[...], k, (((1,), (1,)), ((), ())), preferred_element_type=F32) * scale
    m_prev = m_sc[...]
    m_new = jnp.maximum(m_prev, jnp.max(s, axis=1, keepdims=True))
    alpha = jnp.exp(m_prev - m_new)
    p = jnp.exp(s - m_new)
    l_sc[...] = alpha * l_sc[...] + jnp.sum(p, axis=1, keepdims=True)
    acc_sc[...] = alpha * acc_sc[...] + jnp.dot(p.astype(BF16), v_ref[...], preferred_element_type=F32)
    m_sc[...] = m_new

    @pl.when(j == nk - 1)
    def _():
        o_ref[...] = (acc_sc[...] / l_sc[...]).astype(o_ref.dtype)


def mla_attention(q, kv, krp, tables, o_prev, *, q_start, q_len, k_start, k_len):
    B, S, _ = q.shape
    H = MLA_HEADS
    tq = _pick(q_len, (512, 256))
    tk = _pick(k_len, (768, 512, 256))
    assert q_start % tq == 0 and k_start % tk == 0
    qo, ko = q_start // tq, k_start // tk
    nq, nk = q_len // tq, k_len // tk
    scale = (MLA_NOPE + MLA_ROPE) ** -0.5
    tab = pl.BlockSpec((tq, LANES), lambda b, h, i, j: (i + qo, 0))
    in_specs = [
        pl.BlockSpec((None, tq, LANES), lambda b, h, i, j: (b, i + qo, h)),
        pl.BlockSpec((None, tq, LANES), lambda b, h, i, j: (b, i + qo, H + h)),
        tab, tab, tab,
        pl.BlockSpec((None, tk, LANES), lambda b, h, i, j: (b, j + ko, h)),
        pl.BlockSpec((None, tk, LANES), lambda b, h, i, j: (b, j + ko, 0)),
        pl.BlockSpec((None, tk, LANES), lambda b, h, i, j: (b, j + ko, H + h)),
    ]
    args = [q, q, *tables, kv, krp, kv]
    aliases = {}
    if o_prev is not None:
        in_specs.append(pl.BlockSpec(memory_space=pl.ANY))
        args.append(o_prev)
        aliases = {len(args) - 1: 0}

    def body(*refs):
        if o_prev is not None:
            refs = refs[:8] + refs[9:]
        _mla_attn_kernel(*refs, scale=scale, nk=nk)

    return pl.pallas_call(
        body,
        grid=(B, H, nq, nk),
        in_specs=in_specs,
        out_specs=pl.BlockSpec((None, tq, LANES), lambda b, h, i, j: (b, i + qo, h)),
        out_shape=jax.ShapeDtypeStruct((B, S, H * MLA_V), BF16),
        scratch_shapes=[pltpu.VMEM((tq, 2 * LANES), BF16), pltpu.VMEM((tq, 1), F32),
                        pltpu.VMEM((tq, 1), F32), pltpu.VMEM((tq, MLA_V), F32)],
        input_output_aliases=aliases,
        compiler_params=_cparams(("parallel", "parallel", "parallel", "arbitrary")),
        name="mla_attn",
    )(*args)


def _softmax_pv(s_list, v_list, sink):
    m = sink
    for s in s_list:
        m = jnp.maximum(m, jnp.max(s, axis=1, keepdims=True))
    den = jnp.exp(sink - m)
    o = None
    for s, v in zip(s_list, v_list):
        p = jnp.exp(s - m)
        den = den + jnp.sum(p, axis=1, keepdims=True)
        pv = jnp.dot(p.astype(BF16), v, preferred_element_type=F32)
        o = pv if o is None else o + pv
    return o / den


def _qk(q, k):
    return lax.dot_general(q, k, (((1,), (1,)), ((), ())), preferred_element_type=F32)


def _win_attn_kernel(sink_ref, q_ref, kp_ref, kc_ref, kn_ref, kx_ref, vp_ref, vc_ref, vn_ref, vx_ref,
                     o_ref, *, scale, nblk):
    p_idx = pl.program_id(1)
    i = pl.program_id(2)
    r = lax.broadcasted_iota(jnp.int32, (WINDOW, WINDOW), 0)
    c = lax.broadcasted_iota(jnp.int32, (WINDOW, WINDOW), 1)
    prev_ok = jnp.logical_and(c >= r, i > 0)
    next_ok = jnp.logical_and(c <= r, i < nblk - 1)
    hd = W_HEAD_DIM
    for hh in range(8):
        kvh = hh // 4
        ks = slice(kvh * hd, (kvh + 1) * hd)
        q = q_ref[:, hh * hd:(hh + 1) * hd]
        sink = jnp.full((1, 1), sink_ref[p_idx * 8 + hh], F32)
        s_p = jnp.where(prev_ok, _qk(q, kp_ref[:, ks]) * scale, NEG_INF)
        s_c = _qk(q, kc_ref[:, ks]) * scale
        s_n = jnp.where(next_ok, _qk(q, kn_ref[:, ks]) * scale, NEG_INF)
        s_x = _qk(q, kx_ref[:, ks]) * scale
        o = _softmax_pv([s_p, s_c, s_n, s_x],
                        [vp_ref[:, ks], vc_ref[:, ks], vn_ref[:, ks], vx_ref[:, ks]], sink)
        o_ref[:, hh * hd:(hh + 1) * hd] = o.astype(o_ref.dtype)


def _ctx_attn_kernel(sink_ref, q_ref, kx_ref, vx_ref, o_ref, *, scale):
    p_idx = pl.program_id(1)
    hd = W_HEAD_DIM
    for hh in range(8):
        kvh = hh // 4
        ks = slice(kvh * hd, (kvh + 1) * hd)
        q = q_ref[:, hh * hd:(hh + 1) * hd]
        sink = jnp.full((1, 1), sink_ref[p_idx * 8 + hh], F32)
        s_x = _qk(q, kx_ref[:, ks]) * scale
        o = _softmax_pv([s_x], [vx_ref[:, ks]], sink)
        o_ref[:, hh * hd:(hh + 1) * hd] = o.astype(o_ref.dtype)


def window_attention(qk, qkv, sinks, n_lat, n_ctx):
    B, S, _ = qkv.shape
    W = WINDOW
    nblk = n_lat // W
    scale = W_HEAD_DIM ** -0.5
    kcol = WQ_HEADS * W_HEAD_DIM // LANES
    vcol = (WQ_HEADS + WKV_HEADS) * W_HEAD_DIM // LANES
    assert n_lat % n_ctx == 0
    xblk = n_lat // n_ctx
    qw = 8 * W_HEAD_DIM
    smem = pl.BlockSpec(memory_space=pltpu.SMEM)

    def kspec(col, shift):
        return pl.BlockSpec((None, W, LANES),
                            lambda b, p, i: (b, jnp.clip(i + shift, 0, nblk - 1), col + p))

    def xspec(col):
        return pl.BlockSpec((None, n_ctx, LANES), lambda b, p, i: (b, xblk, col + p))

    o_lat = pl.pallas_call(
        functools.partial(_win_attn_kernel, scale=scale, nblk=nblk),
        grid=(B, WKV_HEADS // 2, nblk),
        in_specs=[smem, pl.BlockSpec((None, W, qw), lambda b, p, i: (b, i, p)),
                  kspec(kcol, -1), kspec(kcol, 0), kspec(kcol, 1), xspec(kcol),
                  kspec(vcol, -1), kspec(vcol, 0), kspec(vcol, 1), xspec(vcol)],
        out_specs=pl.BlockSpec((None, W, qw), lambda b, p, i: (b, i, p)),
        out_shape=jax.ShapeDtypeStruct((B, S, WQ_HEADS * W_HEAD_DIM), BF16),
        compiler_params=_cparams(("parallel", "parallel", "parallel")),
        name="win_attn",
    )(sinks, qk, qk, qk, qk, qk, qkv, qkv, qkv, qkv)

    def ctx_body(sink_ref, q_ref, kx_ref, vx_ref, _, o_ref):
        _ctx_attn_kernel(sink_ref, q_ref, kx_ref, vx_ref, o_ref, scale=scale)

    return pl.pallas_call(
        ctx_body,
        grid=(B, WKV_HEADS // 2, 1),
        in_specs=[smem, pl.BlockSpec((None, n_ctx, qw), lambda b, p, i: (b, xblk, p)),
                  xspec(kcol), xspec(vcol), pl.BlockSpec(memory_space=pl.ANY)],
        out_specs=pl.BlockSpec((None, n_ctx, qw), lambda b, p, i: (b, xblk, p)),
        out_shape=jax.ShapeDtypeStruct((B, S, WQ_HEADS * W_HEAD_DIM), BF16),
        input_output_aliases={4: 0},
        compiler_params=_cparams(("parallel", "parallel", "parallel")),
        name="win_attn_ctx",
    )(sinks, qk, qk, qkv, o_lat)


HALO = 16


def _conv_kernel(x_ref, xp_ref, xn_ref, w_ref, b_ref, dtr_ref, dtb_ref, o_ref, dt_ref, ext_sc,
                 *, lat_tiles, tiles):
    t = pl.program_id(1)
    tr = x_ref.shape[0]
    at_start = jnp.logical_or(t == 0, t == lat_tiles)
    at_end = jnp.logical_or(t == lat_tiles - 1, t == tiles - 1)
    ext_sc[0:HALO, :] = jnp.where(at_start, 0.0, xp_ref[...].astype(F32))
    ext_sc[HALO:HALO + tr, :] = x_ref[...].astype(F32)
    ext_sc[HALO + tr:, :] = jnp.where(at_end, 0.0, xn_ref[...].astype(F32))
    pad = (SSM_CONV - 1) // 2
    acc = jnp.broadcast_to(b_ref[...], (tr, x_ref.shape[1]))
    for k in range(SSM_CONV):
        acc = acc + w_ref[k:k + 1, :] * ext_sc[HALO - pad + k:HALO - pad + k + tr, :]
    o_ref[...] = (acc * (1.0 / (1.0 + jnp.exp(-acc)))).astype(o_ref.dtype)
    z = dtr_ref[...] + dtb_ref[...]
    dt_ref[...] = jnp.maximum(z, 0.0) + jnp.log(1.0 + jnp.exp(-jnp.abs(z)))


def ssd_conv(xbc, dt_raw, conv_w, conv_b, dt_bias, n_lat):
    B, S, C = xbc.shape
    tr = ROW_TILE
    lat_tiles, tiles = n_lat // tr, S // tr
    hb = tr // HALO
    nh = S // HALO
    row = pl.BlockSpec((None, tr, C), lambda b, t: (b, t, 0))
    dtrow = pl.BlockSpec((None, tr, LANES), lambda b, t: (b, t, 0))
    return pl.pallas_call(
        functools.partial(_conv_kernel, lat_tiles=lat_tiles, tiles=tiles),
        grid=(B, tiles),
        in_specs=[row,
                  pl.BlockSpec((None, HALO, C), lambda b, t: (b, jnp.maximum(t * hb - 1, 0), 0)),
                  pl.BlockSpec((None, HALO, C), lambda b, t: (b, jnp.minimum((t + 1) * hb, nh - 1), 0)),
                  pl.BlockSpec((SSM_CONV, C), lambda b, t: (0, 0)),
                  pl.BlockSpec((1, C), lambda b, t: (0, 0)),
                  dtrow,
                  pl.BlockSpec((1, LANES), lambda b, t: (0, 0))],
        out_specs=[row, dtrow],
        out_shape=[jax.ShapeDtypeStruct((B, S, C), BF16), jax.ShapeDtypeStruct((B, S, LANES), F32)],
        scratch_shapes=[pltpu.VMEM((tr + 2 * HALO, C), F32)],
        compiler_params=_cparams(("parallel", "parallel")),
        name="ssd_conv",
    )(xbc, xbc, xbc, conv_w, conv_b.reshape(1, C), dt_raw, dt_bias.reshape(1, LANES))


def _ssd_scan_kernel(x_ref, b_ref, c_ref, dt_ref, dtt_ref, a_ref, at_ref, y_ref, st_sc):
    d = pl.program_id(1)
    s = pl.program_id(3)
    Q = SSM_CHUNK
    P = SSM_HEADDIM

    @pl.when(s == 0)
    def _():
        st_sc[...] = jnp.zeros(st_sc.shape, F32)

    sgn = 1 - 2 * d
    ri = lax.broadcasted_iota(jnp.int32, (Q, Q), 0)
    ci = lax.broadcasted_iota(jnp.int32, (Q, Q), 1)
    mask = (ri - ci) * sgn >= 0
    tri = mask.astype(F32)
    tri_t = ((ci - ri) * sgn >= 0).astype(F32)

    dt = dt_ref[...]
    a = dt * a_ref[...]
    a_t = dtt_ref[...] * at_ref[...]
    hi = lax.Precision.HIGHEST
    acs = jnp.dot(tri, a, preferred_element_type=F32, precision=hi)
    acs_t = jnp.dot(a_t, tri_t, preferred_element_type=F32, precision=hi)
    tot = jnp.sum(a, axis=0, keepdims=True)

    bq = b_ref[...]
    cq = c_ref[...]
    cb = _qk(cq, bq)
    for hh in range(SSM_HG):
        xh = x_ref[:, hh * P:(hh + 1) * P].astype(F32)
        acs_c = acs[:, hh:hh + 1]
        dt_c = dt[:, hh:hh + 1]
        tot_h = tot[:, hh:hh + 1]
        seg = acs_c - acs_t[hh:hh + 1, :]
        decay = jnp.exp(jnp.where(mask, seg, -jnp.inf))
        m = (cb * decay).astype(BF16)
        y = jnp.dot(m, (xh * dt_c).astype(BF16), preferred_element_type=F32)
        st = st_sc[hh]
        y_off = _qk(cq, st.astype(BF16))
        y_ref[:, hh * P:(hh + 1) * P] = (y + y_off * jnp.exp(acs_c)).astype(y_ref.dtype)
        w = jnp.exp(tot_h - acs_c) * dt_c
        upd = lax.dot_general((xh * w).astype(BF16), bq, (((0,), (0,)), ((), ())),
                              preferred_element_type=F32)
        st_sc[hh] = jnp.exp(tot_h) * st + upd


def ssd_scan(xbc_act, dt5, dtt5, a_row, a_col, n_lat):
    B, S, _ = xbc_act.shape
    Q = SSM_CHUNK
    G, HG, P, N = SSM_GROUPS, SSM_HG, SSM_HEADDIM, SSM_STATE
    nch = S // Q
    latc = n_lat // Q
    ncc = nch - latc
    gw = HG * P
    bcol = SSM_D_INNER // N
    ccol = bcol + G

    def chunk(d, s):
        fwd = jnp.where(s < ncc, latc + s, s - ncc)
        return jnp.where(d == 0, fwd, nch - 1 - s)

    return pl.pallas_call(
        _ssd_scan_kernel,
        grid=(B, 2, G, nch),
        in_specs=[
            pl.BlockSpec((None, Q, gw), lambda b, d, g, s: (b, chunk(d, s), g)),
            pl.BlockSpec((None, Q, N), lambda b, d, g, s: (b, chunk(d, s), bcol + g)),
            pl.BlockSpec((None, Q, N), lambda b, d, g, s: (b, chunk(d, s), ccol + g)),
            pl.BlockSpec((None, None, None, Q, HG), lambda b, d, g, s: (b, d, g, chunk(d, s), 0)),
            pl.BlockSpec((None, None, None, HG, Q), lambda b, d, g, s: (b, d, g, 0, chunk(d, s))),
            pl.BlockSpec((None, None, 1, HG), lambda b, d, g, s: (d, g, 0, 0)),
            pl.BlockSpec((None, None, HG, 1), lambda b, d, g, s: (d, g, 0, 0)),
        ],
        out_specs=pl.BlockSpec((None, None, Q, gw), lambda b, d, g, s: (b, d, chunk(d, s), g)),
        out_shape=jax.ShapeDtypeStruct((B, 2, S, SSM_D_INNER), BF16),
        scratch_shapes=[pltpu.VMEM((HG, P, N), F32)],
        compiler_params=_cparams(("parallel", "parallel", "parallel", "arbitrary")),
        name="ssd_scan",
    )(xbc_act, xbc_act, xbc_act, dt5, dtt5, a_row, a_col)


def _ssd_finish_kernel(yf_ref, yb_ref, x_ref, z_ref, d_ref, g_ref, o_ref):
    y = yf_ref[...].astype(F32) + yb_ref[...].astype(F32) + d_ref[...] * x_ref[...].astype(F32)
    z = z_ref[...].astype(F32)
    y = y * (z * (1.0 / (1.0 + jnp.exp(-z))))
    o_ref[...] = _rms(y, g_ref[...]).astype(o_ref.dtype)


def ssd_finish(y2, xbc_act, z, d_full, norm_g):
    B, _, S, C = y2.shape
    tr = ROW_TILE
    row = pl.BlockSpec((None, tr, C), lambda b, t: (b, t, 0))
    vec = pl.BlockSpec((1, C), lambda b, t: (0, 0))
    return pl.pallas_call(
        _ssd_finish_kernel,
        grid=(B, S // tr),
        in_specs=[pl.BlockSpec((None, None, tr, C), lambda b, t: (b, 0, t, 0)),
                  pl.BlockSpec((None, None, tr, C), lambda b, t: (b, 1, t, 0)),
                  row, row, vec, vec],
        out_specs=row,
        out_shape=jax.ShapeDtypeStruct((B, S, C), BF16),
        compiler_params=_cparams(("parallel", "parallel")),
        name="ssd_finish",
    )(y2, y2, xbc_act, z, d_full.reshape(1, C), norm_g.reshape(1, C))


def _extract_topk(x, k):
    n = x.shape[0]
    iota = lax.broadcasted_iota(jnp.int32, x.shape, 0)
    work = x
    member = jnp.zeros(x.shape, F32)
    vals = []
    for _ in range(k):
        m = jnp.max(work, axis=0, keepdims=True)
        idx = jnp.min(jnp.where(work == m, iota, n), axis=0, keepdims=True)
        sel = iota == idx
        work = jnp.where(sel, -jnp.inf, work)
        member = jnp.where(sel, 1.0, member)
        vals.append(m)
    return jnp.concatenate(vals, axis=0), member


_STAIR = [(a, b) for a in range(PEER_TOPK) for b in range(PEER_TOPK) if (a + 1) * (b + 1) <= PEER_TOPK]


def _peer_route_kernel(q_ref, k1_ref, k2_ref, s1_ref, s2_ref, e1_ref, e2_ref, tau_ref):
    for hd in range(PEER_HEADS):
        q1 = q_ref[:, (2 * hd) * PEER_HALF:(2 * hd + 1) * PEER_HALF]
        q2 = q_ref[:, (2 * hd + 1) * PEER_HALF:(2 * hd + 2) * PEER_HALF]
        s1 = _qk(k1_ref[hd], q1)
        s2 = _qk(k2_ref[hd], q2)
        v1, m1 = _extract_topk(s1, PEER_TOPK)
        v2, m2 = _extract_topk(s2, PEER_TOPK)
        pad_rows = [jnp.full((-len(_STAIR) % 8, v1.shape[1]), -jnp.inf, F32)]
        cand = jnp.concatenate([v1[a:a + 1, :] + v2[b:b + 1, :] for a, b in _STAIR] + pad_rows, axis=0)
        top, sel = _extract_topk(cand, PEER_TOPK)
        tau = top[PEER_TOPK - 1:PEER_TOPK, :]
        cmax = top[0:1, :]
        z = jnp.sum(sel * jnp.exp(cand - cmax), axis=0, keepdims=True)
        e1 = m1 * jnp.exp(s1 - v1[0:1, :]) / z
        e2 = m2 * jnp.exp(s2 - v2[0:1, :])
        s1_ref[hd] = s1
        s2_ref[hd] = s2
        e1_ref[hd] = e1
        e2_ref[hd] = e2
        tau_ref[hd] = tau


def peer_route(q, k1, k2):
    M = q.shape[0]
    tm = _pick(M, (256, 128))
    H, K = PEER_HEADS, PEER_N_KEYS
    big = pl.BlockSpec((H, K, tm), lambda i: (0, 0, i))
    kspec = pl.BlockSpec((H, K, PEER_HALF), lambda i: (0, 0, 0))
    sd = jax.ShapeDtypeStruct((H, K, M), F32)
    return pl.pallas_call(
        _peer_route_kernel,
        grid=(M // tm,),
        in_specs=[pl.BlockSpec((tm, 2 * H * PEER_HALF), lambda i: (i, 0)), kspec, kspec],
        out_specs=[big, big, big, big, pl.BlockSpec((H, 1, tm), lambda i: (0, 0, i))],
        out_shape=[sd, sd, sd, sd, jax.ShapeDtypeStruct((H, 1, M), F32)],
        compiler_params=_cparams(("parallel",)),
        name="peer_route",
    )(q, k1, k2)


def _gelu_tanh(x):
    return 0.5 * x * (1.0 + jnp.tanh(0.7978845608028654 * (x + 0.044715 * (x * x * x))))


def _peer_dense_kernel(h_ref, u_ref, vt_ref, s1_ref, s2_ref, e1_ref, e2_ref, tau_ref, o_ref,
                       acc_sc, p_sc, *, ne):
    j = pl.program_id(1)
    te = u_ref.shape[0]
    kper = te // PEER_N_KEYS

    @pl.when(j == 0)
    def _():
        acc_sc[...] = jnp.zeros(acc_sc.shape, F32)

    for kk in range(kper):
        k1 = j * kper + kk
        rows = slice(kk * PEER_N_KEYS, (kk + 1) * PEER_N_KEYS)
        act = _gelu_tanh(_qk(u_ref[rows, :], h_ref[...]))
        g = jnp.zeros(act.shape, F32)
        for hd in range(PEER_HEADS):
            s1r = s1_ref[hd, pl.ds(k1, 1), :]
            e1r = e1_ref[hd, pl.ds(k1, 1), :]
            cand = s1r + s2_ref[hd]
            g = g + jnp.where(cand >= tau_ref[hd], e1r * e2_ref[hd], 0.0)
        p_sc[rows, :] = (g * act).astype(BF16)
    acc_sc[...] += jnp.dot(vt_ref[...], p_sc[...], preferred_element_type=F32)

    @pl.when(j == ne - 1)
    def _():
        o_ref[...] = acc_sc[...].T


def peer_dense(h, u, vt, s1, s2, e1, e2, tau):
    M, D = h.shape
    E = u.shape[0]
    tm = _pick(M, (512, 384, 256, 128))
    te = 512
    ne = E // te
    H, K = PEER_HEADS, PEER_N_KEYS
    big = pl.BlockSpec((H, K, tm), lambda i, j: (0, 0, i))
    return pl.pallas_call(
        functools.partial(_peer_dense_kernel, ne=ne),
        grid=(M // tm, ne),
        in_specs=[pl.BlockSpec((tm, D), lambda i, j: (i, 0)),
                  pl.BlockSpec((te, D), lambda i, j: (j, 0)),
                  pl.BlockSpec((D, te), lambda i, j: (0, j)),
                  big, big, big, big,
                  pl.BlockSpec((H, 1, tm), lambda i, j: (0, 0, i))],
        out_specs=pl.BlockSpec((tm, D), lambda i, j: (i, 0)),
        out_shape=jax.ShapeDtypeStruct((M, D), F32),
        scratch_shapes=[pltpu.VMEM((D, tm), F32), pltpu.VMEM((te, tm), BF16)],
        compiler_params=_cparams(("parallel", "arbitrary")),
        name="peer_dense",
    )(h, u, vt, s1, s2, e1, e2, tau)


def peer_ffn(h, w_q, k1, k2, u, vt):
    q = matmul(h, w_q, BF16, "peer_q")
    s1, s2, e1, e2, tau = peer_route(q, k1, k2)
    return peer_dense(h, u, vt, s1, s2, e1, e2, tau)


def _mla_weights(w_a, w_uq, w_ukv):
    H, dn, dr, dv = MLA_HEADS, MLA_NOPE, MLA_ROPE, MLA_V
    pad_a = jnp.zeros((w_a.shape[0], LANES - dr), w_a.dtype)
    w_a_p = jnp.concatenate([w_a, pad_a], axis=1).astype(BF16)
    wq = w_uq.reshape(MLA_Q_RANK, H, dn + dr)
    wq_n = wq[:, :, :dn].reshape(MLA_Q_RANK, H * dn)
    wq_r = jnp.concatenate([wq[:, :, dn:], jnp.zeros((MLA_Q_RANK, H, LANES - dr), wq.dtype)], axis=2)
    w_q_p = jnp.concatenate([wq_n, wq_r.reshape(MLA_Q_RANK, H * LANES)], axis=1).astype(BF16)
    wkv = w_ukv.reshape(MLA_KV_RANK, H, dn + dv)
    w_kv_p = jnp.concatenate([wkv[:, :, :dn].reshape(MLA_KV_RANK, H * dn),
                              wkv[:, :, dn:].reshape(MLA_KV_RANK, H * dv)], axis=1).astype(BF16)
    return w_a_p, w_q_p, w_kv_p


def mla_mixer(h, w_a, q_g, kv_g, w_uq, w_ukv, w_o, tables, n_lat, n_ctx):
    B, S, D = h.shape
    M = B * S
    w_a_p, w_q_p, w_kv_p = _mla_weights(w_a, w_uq, w_ukv)
    ca = matmul(h.reshape(M, D), w_a_p, F32, "mla_a").reshape(B, S, -1)
    qc, kvc, krp = mla_prep(ca, q_g, kv_g, tables)
    q = matmul(qc.reshape(M, -1), w_q_p, BF16, "mla_q").reshape(B, S, -1)
    kv = matmul(kvc.reshape(M, -1), w_kv_p, BF16, "mla_kv").reshape(B, S, -1)
    o = mla_attention(q, kv, krp, tables, None, q_start=0, q_len=n_lat, k_start=0, k_len=S)
    o = mla_attention(q, kv, krp, tables, o, q_start=n_lat, q_len=n_ctx, k_start=n_lat, k_len=n_ctx)
    return matmul(o.reshape(M, -1), w_o.astype(BF16), F32, "mla_o").reshape(B, S, D)


def ssd_mixer(h, w_in, conv_w, conv_b, a_log, dt_bias, d_skip, norm_g, w_out, n_lat):
    B, S, D = h.shape
    M = B * S
    G, HG = SSM_GROUPS, SSM_HG
    h2 = h.reshape(M, D)
    w_bf = w_in.astype(BF16)
    z = matmul(h2, w_bf[:, :SSM_D_INNER], BF16, "ssd_in_z").reshape(B, S, -1)
    xbc = matmul(h2, w_bf[:, SSM_D_INNER:SSM_D_INNER + SSM_CONV_DIM], BF16, "ssd_in_x").reshape(B, S, -1)
    dt_raw = matmul(h2, w_bf[:, SSM_D_INNER + SSM_CONV_DIM:], F32, "ssd_in_dt").reshape(B, S, -1)
    xbc_act, dt = ssd_conv(xbc, dt_raw, conv_w, conv_b, dt_bias.reshape(-1), n_lat)
    dt5 = dt.reshape(B, S, 2, G, HG).transpose(0, 2, 3, 1, 4)
    dtt5 = dt5.transpose(0, 1, 2, 4, 3)
    a = -jnp.exp(a_log.astype(F32)).reshape(2, G, HG)
    y2 = ssd_scan(xbc_act, dt5, dtt5, a[:, :, None, :], a[:, :, :, None], n_lat)
    d_full = jnp.repeat(d_skip.astype(F32), SSM_HEADDIM)
    yn = ssd_finish(y2, xbc_act, z, d_full, norm_g)
    return matmul(yn.reshape(M, -1), w_out.astype(BF16), F32, "ssd_out").reshape(B, S, D)


def window_mixer(h, w_qkv, sinks, w_o, tables, n_lat, n_ctx):
    B, S, D = h.shape
    M = B * S
    qkv = matmul(h.reshape(M, D), w_qkv.astype(BF16), BF16, "win_qkv").reshape(B, S, -1)
    qk = rope_cols(qkv, (WQ_HEADS + WKV_HEADS) * W_HEAD_DIM, tables)
    o = window_attention(qk, qkv, sinks.astype(F32), n_lat, n_ctx)
    return matmul(o.reshape(M, -1), w_o.astype(BF16), F32, "win_o").reshape(B, S, D)


def kernel(x, c, ctx, c_ctx, ada_w, ada_b, norm_mix_g, norm_ffn_g, mla_w_a, mla_q_norm_g, mla_kv_norm_g,
           mla_w_uq, mla_w_ukv, mla_w_o, ssd_w_in, ssd_conv_w, ssd_conv_b, ssd_a_log, ssd_dt_bias,
           ssd_d_skip, ssd_norm_g, ssd_w_out, win_w_qkv, win_sinks, win_w_o, peer_w_q, peer_k1, peer_k2,
           peer_u, peer_v, final_norm_g):
    B, L, D = x.shape
    Lc = ctx.shape[1]
    S = L + Lc
    M = B * S
    depth = ada_w.shape[0]
    xs = jnp.concatenate([x, ctx], axis=1)

    n_grp = 8
    cmat = jnp.zeros((n_grp, D), F32).at[:B].set(c).at[B].set(c_ctx)
    mod_all = adaln_all(cmat, ada_w, ada_b).reshape(depth, n_grp, ADA_CHUNKS, 1, D)

    tab_mla = rope_tables(L, Lc, repeat_upper=False)
    tab_win = rope_tables(L, Lc, repeat_upper=True)

    h = None
    ffn = None
    for i in range(depth):
        mod = mod_all[i]
        if i == 0:
            (h,) = resid_norm_mod(xs, None, None, None, norm_mix_g[i], mod, 0, 1, L)
        else:
            xs, h = resid_norm_mod(xs, ffn, mod_all[i - 1], 5, norm_mix_g[i], mod, 0, 1, L)
        kind, j = i % N_MIXERS, i // N_MIXERS
        if kind == 0:
            mix = mla_mixer(h, mla_w_a[j], mla_q_norm_g[j], mla_kv_norm_g[j], mla_w_uq[j], mla_w_ukv[j],
                            mla_w_o[j], tab_mla, L, Lc)
        elif kind == 1:
            mix = ssd_mixer(h, ssd_w_in[j], ssd_conv_w[j], ssd_conv_b[j], ssd_a_log[j], ssd_dt_bias[j],
                            ssd_d_skip[j], ssd_norm_g[j], ssd_w_out[j], L)
        else:
            mix = window_mixer(h, win_w_qkv[j], win_sinks[j], win_w_o[j], tab_win, L, Lc)
        xs, h2 = resid_norm_mod(xs, mix, mod, 2, norm_ffn_g[i], mod, 3, 4, L)
        ffn = peer_ffn(h2.reshape(M, D), peer_w_q[i].astype(BF16),
                       peer_k1[i].astype(BF16), peer_k2[i].astype(BF16),
                       peer_u[i].astype(BF16), peer_v[i].T.astype(BF16)).reshape(B, S, D)
    (out,) = resid_norm_mod(xs, ffn, mod_all[depth - 1], 5, final_norm_g, None, None, None, L, final=True)
    return out
```

```python
import functools
import math

import jax
import jax.numpy as jnp
from jax import lax
from jax.experimental import pallas as pl
from jax.experimental.pallas import tpu as pltpu

F32 = jnp.float32
BF16 = jnp.bfloat16

D_MODEL = 2048
DEPTH = 4
GRID_W = 64
N_MIXERS = 3
NORM_EPS = 1e-6
ROPE_THETA = 10000.0
NEG_INF = -1e30
ADA_CHUNKS = 6

MLA_HEADS = 16
MLA_Q_RANK = 512
MLA_KV_RANK = 512
MLA_NOPE = 128
MLA_ROPE = 64
MLA_V = 128
MLA_VT_ROWS = MLA_V + 16

SSM_D_INNER = 2 * D_MODEL
SSM_HEADDIM = 64
SSM_HEADS = SSM_D_INNER // SSM_HEADDIM
SSM_GROUPS = 8
SSM_HG = SSM_HEADS // SSM_GROUPS
SSM_STATE = 128
SSM_CONV = 5
SSM_CHUNK = 128
SSM_BC = SSM_GROUPS * SSM_STATE
SSM_CONV_DIM = SSM_D_INNER + 2 * SSM_BC

WINDOW = 128
WQ_HEADS = 32
WKV_HEADS = 8
W_HEAD_DIM = 64

PEER_HEADS = 8
PEER_N_KEYS = 128
PEER_N_EXPERTS = PEER_N_KEYS * PEER_N_KEYS
PEER_TOPK = 16
PEER_HALF = 128

LANES = 128
VMEM_LIMIT = 56 * 1024 * 1024
ROW_TILE = 256


def _pick(n, candidates):
    for c in candidates:
        if n % c == 0:
            return c
    raise ValueError(f"no tile for {n} in {candidates}")


def _cparams(sem):
    return pltpu.CompilerParams(dimension_semantics=sem, vmem_limit_bytes=VMEM_LIMIT)


def _mm_kernel(a_ref, w_ref, o_ref):
    o_ref[...] = jnp.dot(a_ref[...], w_ref[...], preferred_element_type=F32).astype(o_ref.dtype)


def matmul(a, w, out_dtype, name):
    M, K = a.shape
    N = w.shape[1]
    tm = _pick(M, (1056, 1024, 768, 512, 384, 256, 128, 64, 32, 16))
    tn = _pick(N, (512, 384, 256, 128))
    return pl.pallas_call(
        _mm_kernel,
        grid=(M // tm, N // tn),
        in_specs=[pl.BlockSpec((tm, K), lambda i, j: (i, 0)),
                  pl.BlockSpec((K, tn), lambda i, j: (0, j))],
        out_specs=pl.BlockSpec((tm, tn), lambda i, j: (i, j)),
        out_shape=jax.ShapeDtypeStruct((M, N), out_dtype),
        compiler_params=_cparams(("parallel", "parallel")),
        name=name,
    )(a, w)


def _adaln_kernel(c_ref, w_ref, b_ref, o_ref):
    c = c_ref[...]
    s = c * (1.0 / (1.0 + jnp.exp(-c)))
    o_ref[...] = jnp.dot(s, w_ref[...], preferred_element_type=F32,
                         precision=lax.Precision.HIGHEST) + b_ref[...]


def adaln_all(cmat, ada_w, ada_b):
    depth, d, n = ada_w.shape
    rows = cmat.shape[0]
    tn = _pick(n, (1024, 512, 256, 128))
    return pl.pallas_call(
        _adaln_kernel,
        grid=(depth, n // tn),
        in_specs=[pl.BlockSpec((rows, d), lambda l, j: (0, 0)),
                  pl.BlockSpec((None, d, tn), lambda l, j: (l, 0, j)),
                  pl.BlockSpec((None, 1, tn), lambda l, j: (l, 0, j))],
        out_specs=pl.BlockSpec((None, rows, tn), lambda l, j: (l, 0, j)),
        out_shape=jax.ShapeDtypeStruct((depth, rows, n), F32),
        compiler_params=_cparams(("parallel", "parallel")),
        name="adaln",
    )(cmat, ada_w, ada_b.reshape(depth, 1, n))


def _rms(x, g):
    var = jnp.mean(x * x, axis=-1, keepdims=True)
    return x * lax.rsqrt(var + NORM_EPS) * g


def _rnm_kernel(*refs, has_mix, modulate):
    it = iter(refs)
    x_ref = next(it)
    if has_mix:
        mix_ref, gate_ref = next(it), next(it)
    g_ref = next(it)
    if modulate:
        shift_ref, scale_ref = next(it), next(it)
    if has_mix and modulate:
        xo_ref = next(it)
    h_ref = next(it)
    x = x_ref[...]
    if has_mix:
        x = x + gate_ref[...] * mix_ref[...].astype(F32)
        if modulate:
            xo_ref[...] = x
    y = _rms(x, g_ref[...])
    if modulate:
        y = y * (1.0 + scale_ref[...]) + shift_ref[...]
    h_ref[...] = y.astype(h_ref.dtype)


def resid_norm_mod(x, mix, gate_mod, gate_k, norm_g, mod, shift_k, scale_k, n_lat, *, final=False):
    B, S, D = x.shape
    tr = ROW_TILE
    lat_tiles = n_lat // tr
    tiles = lat_tiles if final else S // tr
    has_mix = mix is not None
    modulate = not final

    def grp(b, t):
        return jnp.where(t < lat_tiles, b, B)

    row = pl.BlockSpec((None, tr, D), lambda b, t: (b, t, 0))

    def modspec(k):
        return pl.BlockSpec((None, None, 1, D), lambda b, t: (grp(b, t), k, 0, 0))

    args, specs = [x], [row]
    if has_mix:
        args += [mix, gate_mod]
        specs += [row, modspec(gate_k)]
    args.append(norm_g.reshape(1, D))
    specs.append(pl.BlockSpec((1, D), lambda b, t: (0, 0)))
    if modulate:
        args += [mod, mod]
        specs += [modspec(shift_k), modspec(scale_k)]
    out_shapes, out_specs = [], []
    if has_mix and modulate:
        out_shapes.append(jax.ShapeDtypeStruct((B, S, D), F32))
        out_specs.append(row)
    if final:
        out_shapes.append(jax.ShapeDtypeStruct((B, n_lat, D), F32))
    else:
        out_shapes.append(jax.ShapeDtypeStruct((B, S, D), BF16))
    out_specs.append(row)
    outs = pl.pallas_call(
        functools.partial(_rnm_kernel, has_mix=has_mix, modulate=modulate),
        grid=(B, tiles),
        in_specs=specs,
        out_specs=out_specs,
        out_shape=out_shapes,
        compiler_params=_cparams(("parallel", "parallel")),
        name="resid_norm_mod",
    )(*args)
    return outs


def rope_tables(n_lat, n_ctx, repeat_upper):
    half = MLA_ROPE // 4
    pos = jnp.arange(n_lat, dtype=jnp.int32)
    rows = (pos // GRID_W).astype(F32)
    cols = (pos % GRID_W).astype(F32)
    freqs = ROPE_THETA ** (-jnp.arange(half, dtype=F32) / half)
    ang_r = rows[:, None] * freqs
    ang_c = cols[:, None] * freqs
    ang = jnp.concatenate([ang_r, ang_r, ang_c, ang_c], axis=-1)
    cos, sin = jnp.cos(ang), jnp.sin(ang)
    first = (jnp.arange(64) % 32) < 16
    sa = jnp.where(first, -sin, 0.0)
    sb = jnp.where(first, 0.0, sin)
    cos = jnp.concatenate([cos, jnp.ones((n_ctx, 64), F32)], axis=0)
    sa = jnp.concatenate([sa, jnp.zeros((n_ctx, 64), F32)], axis=0)
    sb = jnp.concatenate([sb, jnp.zeros((n_ctx, 64), F32)], axis=0)
    if repeat_upper:
        return tuple(jnp.concatenate([t, t], axis=-1) for t in (cos, sa, sb))
    z = jnp.zeros_like(cos)
    return tuple(jnp.concatenate([t, z], axis=-1) for t in (cos, sa, sb))


def _rope128(x, cos, sa, sb):
    return x * cos + pltpu.roll(x, 112, 1) * sa + pltpu.roll(x, 16, 1) * sb


def _rope_cols_kernel(x_ref, cos_ref, sa_ref, sb_ref, o_ref):
    cos, sa, sb = cos_ref[...], sa_ref[...], sb_ref[...]
    for c in range(x_ref.shape[-1] // LANES):
        sl = slice(c * LANES, (c + 1) * LANES)
        o_ref[:, sl] = _rope128(x_ref[:, sl].astype(F32), cos, sa, sb).astype(o_ref.dtype)


def rope_cols(x, width, tables):
    B, S, _ = x.shape
    tr = ROW_TILE
    tab = pl.BlockSpec((tr, LANES), lambda b, t: (t, 0))
    return pl.pallas_call(
        _rope_cols_kernel,
        grid=(B, S // tr),
        in_specs=[pl.BlockSpec((None, tr, width), lambda b, t: (b, t, 0)), tab, tab, tab],
        out_specs=pl.BlockSpec((None, tr, width), lambda b, t: (b, t, 0)),
        out_shape=jax.ShapeDtypeStruct((B, S, width), BF16),
        compiler_params=_cparams(("parallel", "parallel")),
        name="rope_cols",
    )(x, *tables)


def _mla_prep_kernel(ca_ref, qg_ref, kvg_ref, cos_ref, sa_ref, sb_ref, qc_ref, kvc_ref, kr_ref):
    cq = ca_ref[:, :MLA_Q_RANK]
    ckv = ca_ref[:, MLA_Q_RANK:MLA_Q_RANK + MLA_KV_RANK]
    kr = ca_ref[:, MLA_Q_RANK + MLA_KV_RANK:]
    qc_ref[...] = _rms(cq, qg_ref[...]).astype(BF16)
    kvc_ref[...] = _rms(ckv, kvg_ref[...]).astype(BF16)
    kr_ref[...] = _rope128(kr, cos_ref[...], sa_ref[...], sb_ref[...]).astype(BF16)


def mla_prep(ca, q_g, kv_g, tables):
    B, S, C = ca.shape
    tr = ROW_TILE
    tab = pl.BlockSpec((tr, LANES), lambda b, t: (t, 0))

    def row(w):
        return pl.BlockSpec((None, tr, w), lambda b, t: (b, t, 0))

    def vec(w):
        return pl.BlockSpec((1, w), lambda b, t: (0, 0))

    return pl.pallas_call(
        _mla_prep_kernel,
        grid=(B, S // tr),
        in_specs=[row(C), vec(MLA_Q_RANK), vec(MLA_KV_RANK), tab, tab, tab],
        out_specs=[row(MLA_Q_RANK), row(MLA_KV_RANK), row(LANES)],
        out_shape=[jax.ShapeDtypeStruct((B, S, MLA_Q_RANK), BF16),
                   jax.ShapeDtypeStruct((B, S, MLA_KV_RANK), BF16),
                   jax.ShapeDtypeStruct((B, S, LANES), BF16)],
        compiler_params=_cparams(("parallel", "parallel")),
        name="mla_prep",
    )(ca, q_g.reshape(1, -1), kv_g.reshape(1, -1), *tables)


def _mla_attn_kernel(qn_ref, qr_ref, cos_ref, sa_ref, sb_ref, kn_ref, kr_ref, vt_ref, o_ref,
                     q_sc, m_sc, acc_sc, *, qscale, nk, nsub):
    j = pl.program_id(3)
    tqs = q_sc.shape[0] // nsub

    @pl.when(j == 0)
    def _():
        qr = _rope128(qr_ref[...].astype(F32), cos_ref[...], sa_ref[...], sb_ref[...])
        q_sc[:, :LANES] = (qn_ref[...].astype(F32) * qscale).astype(BF16)
        q_sc[:, LANES:] = (qr * qscale).astype(BF16)
        m_sc[...] = jnp.full(m_sc.shape, -jnp.inf, F32)
        acc_sc[...] = jnp.zeros(acc_sc.shape, F32)

    k = jnp.concatenate([kn_ref[...], kr_ref[...]], axis=1)
    scores = [_qk(k, q_sc[u * tqs:(u + 1) * tqs, :]) for u in range(nsub)]
    for u in range(nsub):
        cols = slice(u * tqs, (u + 1) * tqs)
        s = scores[u]
        m_prev = m_sc[:, cols]
        m_new = jnp.maximum(m_prev, jnp.max(s, axis=0, keepdims=True))
        alpha = jnp.exp2(m_prev - m_new)
        p = jnp.exp2((s - m_new).astype(BF16))
        acc_sc[:, cols] = alpha * acc_sc[:, cols] + jnp.dot(vt_ref[...], p, preferred_element_type=F32)
        m_sc[:, cols] = m_new

    @pl.when(j == nk - 1)
    def _():
        o_ref[...] = (acc_sc[:MLA_V, :] / acc_sc[MLA_V:MLA_V + 1, :]).T.astype(o_ref.dtype)


def _qk(q, k):
    return lax.dot_general(q, k, (((1,), (1,)), ((), ())), preferred_element_type=F32)


def mla_attention(q, kv, vt, krp, tables, o_prev, *, q_start, q_len, k_start, k_len):
    B, S, _ = q.shape
    H = MLA_HEADS
    tq = _pick(q_len, (2048, 1024, 512, 256))
    nsub = max(tq // 512, 1)
    tk = _pick(k_len, (768, 512, 256))
    assert q_start % tq == 0 and k_start % tk == 0
    qo, ko = q_start // tq, k_start // tk
    nq, nk = q_len // tq, k_len // tk
    qscale = (MLA_NOPE + MLA_ROPE) ** -0.5 * math.log2(math.e)
    tab = pl.BlockSpec((tq, LANES), lambda b, h, i, j: (i + qo, 0))
    in_specs = [
        pl.BlockSpec((None, tq, LANES), lambda b, h, i, j: (b, i + qo, h)),
        pl.BlockSpec((None, tq, LANES), lambda b, h, i, j: (b, i + qo, H + h)),
        tab, tab, tab,
        pl.BlockSpec((None, tk, LANES), lambda b, h, i, j: (b, j + ko, h)),
        pl.BlockSpec((None, tk, LANES), lambda b, h, i, j: (b, j + ko, 0)),
        pl.BlockSpec((None, None, MLA_VT_ROWS, tk), lambda b, h, i, j: (b, h, 0, j + ko)),
    ]
    args = [q, q, *tables, kv, krp, vt]
    aliases = {}
    if o_prev is not None:
        in_specs.append(pl.BlockSpec(memory_space=pl.ANY))
        args.append(o_prev)
        aliases = {len(args) - 1: 0}

    def body(*refs):
        if o_prev is not None:
            refs = refs[:8] + refs[9:]
        _mla_attn_kernel(*refs, qscale=qscale, nk=nk, nsub=nsub)

    return pl.pallas_call(
        body,
        grid=(B, H, nq, nk),
        in_specs=in_specs,
        out_specs=pl.BlockSpec((None, tq, LANES), lambda b, h, i, j: (b, i + qo, h)),
        out_shape=jax.ShapeDtypeStruct((B, S, H * MLA_V), BF16),
        scratch_shapes=[pltpu.VMEM((tq, 2 * LANES), BF16), pltpu.VMEM((1, tq), F32),
                        pltpu.VMEM((MLA_VT_ROWS, tq), F32)],
        input_output_aliases=aliases,
        compiler_params=_cparams(("parallel", "parallel", "parallel", "arbitrary")),
        name="mla_attn",
    )(*args)


def mla_attention_all(q, kv, krp, tables, n_lat, n_ctx):
    B, S, _ = q.shape
    H = MLA_HEADS
    vt = jnp.swapaxes(kv[:, :, H * MLA_NOPE:], 1, 2).reshape(B, H, MLA_V, S)
    vt = jnp.concatenate([vt, jnp.ones((B, H, MLA_VT_ROWS - MLA_V, S), BF16)], axis=2)
    o = mla_attention(q, kv, vt, krp, tables, None, q_start=0, q_len=n_lat, k_start=0, k_len=S)
    return mla_attention(q, kv, vt, krp, tables, o, q_start=n_lat, q_len=n_ctx, k_start=n_lat, k_len=n_ctx)


def _softmax_pv(s_list, v_list, sink):
    m = sink
    for s in s_list:
        m = jnp.maximum(m, jnp.max(s, axis=1, keepdims=True))
    den = jnp.exp(sink - m)
    o = None
    for s, v in zip(s_list, v_list):
        p = jnp.exp(s - m)
        den = den + jnp.sum(p, axis=1, keepdims=True)
        pv = jnp.dot(p.astype(BF16), v, preferred_element_type=F32)
        o = pv if o is None else o + pv
    return o / den


def _win_attn_kernel(sink_ref, q_ref, kp_ref, kc_ref, kn_ref, kx_ref, vp_ref, vc_ref, vn_ref, vx_ref,
                     o_ref, *, scale, nblk):
    p_idx = pl.program_id(1)
    i = pl.program_id(2)
    r = lax.broadcasted_iota(jnp.int32, (WINDOW, WINDOW), 0)
    c = lax.broadcasted_iota(jnp.int32, (WINDOW, WINDOW), 1)
    prev_ok = jnp.logical_and(c >= r, i > 0)
    next_ok = jnp.logical_and(c <= r, i < nblk - 1)
    hd = W_HEAD_DIM
    for hh in range(8):
        kvh = hh // 4
        ks = slice(kvh * hd, (kvh + 1) * hd)
        q = q_ref[:, hh * hd:(hh + 1) * hd]
        sink = jnp.full((1, 1), sink_ref[p_idx * 8 + hh], F32)
        s_p = jnp.where(prev_ok, _qk(q, kp_ref[:, ks]) * scale, NEG_INF)
        s_c = _qk(q, kc_ref[:, ks]) * scale
        s_n = jnp.where(next_ok, _qk(q, kn_ref[:, ks]) * scale, NEG_INF)
        s_x = _qk(q, kx_ref[:, ks]) * scale
        o = _softmax_pv([s_p, s_c, s_n, s_x],
                        [vp_ref[:, ks], vc_ref[:, ks], vn_ref[:, ks], vx_ref[:, ks]], sink)
        o_ref[:, hh * hd:(hh + 1) * hd] = o.astype(o_ref.dtype)


def _ctx_attn_kernel(sink_ref, q_ref, kx_ref, vx_ref, o_ref, *, scale):
    p_idx = pl.program_id(1)
    hd = W_HEAD_DIM
    for hh in range(8):
        kvh = hh // 4
        ks = slice(kvh * hd, (kvh + 1) * hd)
        q = q_ref[:, hh * hd:(hh + 1) * hd]
        sink = jnp.full((1, 1), sink_ref[p_idx * 8 + hh], F32)
        s_x = _qk(q, kx_ref[:, ks]) * scale
        o = _softmax_pv([s_x], [vx_ref[:, ks]], sink)
        o_ref[:, hh * hd:(hh + 1) * hd] = o.astype(o_ref.dtype)


def window_attention(qk, qkv, sinks, n_lat, n_ctx):
    B, S, _ = qkv.shape
    W = WINDOW
    nblk = n_lat // W
    scale = W_HEAD_DIM ** -0.5
    kcol = WQ_HEADS * W_HEAD_DIM // LANES
    vcol = (WQ_HEADS + WKV_HEADS) * W_HEAD_DIM // LANES
    assert n_lat % n_ctx == 0
    xblk = n_lat // n_ctx
    qw = 8 * W_HEAD_DIM
    smem = pl.BlockSpec(memory_space=pltpu.SMEM)

    def kspec(col, shift):
        return pl.BlockSpec((None, W, LANES),
                            lambda b, p, i: (b, jnp.clip(i + shift, 0, nblk - 1), col + p))

    def xspec(col):
        return pl.BlockSpec((None, n_ctx, LANES), lambda b, p, i: (b, xblk, col + p))

    o_lat = pl.pallas_call(
        functools.partial(_win_attn_kernel, scale=scale, nblk=nblk),
        grid=(B, WKV_HEADS // 2, nblk),
        in_specs=[smem, pl.BlockSpec((None, W, qw), lambda b, p, i: (b, i, p)),
                  kspec(kcol, -1), kspec(kcol, 0), kspec(kcol, 1), xspec(kcol),
                  kspec(vcol, -1), kspec(vcol, 0), kspec(vcol, 1), xspec(vcol)],
        out_specs=pl.BlockSpec((None, W, qw), lambda b, p, i: (b, i, p)),
        out_shape=jax.ShapeDtypeStruct((B, S, WQ_HEADS * W_HEAD_DIM), BF16),
        compiler_params=_cparams(("parallel", "parallel", "parallel")),
        name="win_attn",
    )(sinks, qk, qk, qk, qk, qk, qkv, qkv, qkv, qkv)

    def ctx_body(sink_ref, q_ref, kx_ref, vx_ref, _, o_ref):
        _ctx_attn_kernel(sink_ref, q_ref, kx_ref, vx_ref, o_ref, scale=scale)

    return pl.pallas_call(
        ctx_body,
        grid=(B, WKV_HEADS // 2, 1),
        in_specs=[smem, pl.BlockSpec((None, n_ctx, qw), lambda b, p, i: (b, xblk, p)),
                  xspec(kcol), xspec(vcol), pl.BlockSpec(memory_space=pl.ANY)],
        out_specs=pl.BlockSpec((None, n_ctx, qw), lambda b, p, i: (b, xblk, p)),
        out_shape=jax.ShapeDtypeStruct((B, S, WQ_HEADS * W_HEAD_DIM), BF16),
        input_output_aliases={4: 0},
        compiler_params=_cparams(("parallel", "parallel", "parallel")),
        name="win_attn_ctx",
    )(sinks, qk, qk, qkv, o_lat)


HALO = 16


def _conv_kernel(x_ref, xp_ref, xn_ref, w_ref, b_ref, dtr_ref, dtb_ref, o_ref, dt_ref, ext_sc,
                 *, lat_tiles, tiles):
    t = pl.program_id(1)
    tr = x_ref.shape[0]
    at_start = jnp.logical_or(t == 0, t == lat_tiles)
    at_end = jnp.logical_or(t == lat_tiles - 1, t == tiles - 1)
    ext_sc[0:HALO, :] = jnp.where(at_start, 0.0, xp_ref[...].astype(F32))
    ext_sc[HALO:HALO + tr, :] = x_ref[...].astype(F32)
    ext_sc[HALO + tr:, :] = jnp.where(at_end, 0.0, xn_ref[...].astype(F32))
    pad = (SSM_CONV - 1) // 2
    acc = jnp.broadcast_to(b_ref[...], (tr, x_ref.shape[1]))
    for k in range(SSM_CONV):
        acc = acc + w_ref[k:k + 1, :] * ext_sc[HALO - pad + k:HALO - pad + k + tr, :]
    o_ref[...] = (acc * (1.0 / (1.0 + jnp.exp(-acc)))).astype(o_ref.dtype)
    z = dtr_ref[...] + dtb_ref[...]
    dt_ref[...] = jnp.maximum(z, 0.0) + jnp.log(1.0 + jnp.exp(-jnp.abs(z)))


def ssd_conv(xbc, dt_raw, conv_w, conv_b, dt_bias, n_lat):
    B, S, C = xbc.shape
    tr = ROW_TILE
    lat_tiles, tiles = n_lat // tr, S // tr
    hb = tr // HALO
    nh = S // HALO
    row = pl.BlockSpec((None, tr, C), lambda b, t: (b, t, 0))
    dtrow = pl.BlockSpec((None, tr, LANES), lambda b, t: (b, t, 0))
    return pl.pallas_call(
        functools.partial(_conv_kernel, lat_tiles=lat_tiles, tiles=tiles),
        grid=(B, tiles),
        in_specs=[row,
                  pl.BlockSpec((None, HALO, C), lambda b, t: (b, jnp.maximum(t * hb - 1, 0), 0)),
                  pl.BlockSpec((None, HALO, C), lambda b, t: (b, jnp.minimum((t + 1) * hb, nh - 1), 0)),
                  pl.BlockSpec((SSM_CONV, C), lambda b, t: (0, 0)),
                  pl.BlockSpec((1, C), lambda b, t: (0, 0)),
                  dtrow,
                  pl.BlockSpec((1, LANES), lambda b, t: (0, 0))],
        out_specs=[row, dtrow],
        out_shape=[jax.ShapeDtypeStruct((B, S, C), BF16), jax.ShapeDtypeStruct((B, S, LANES), F32)],
        scratch_shapes=[pltpu.VMEM((tr + 2 * HALO, C), F32)],
        compiler_params=_cparams(("parallel", "parallel")),
        name="ssd_conv",
    )(xbc, xbc, xbc, conv_w, conv_b.reshape(1, C), dt_raw, dt_bias.reshape(1, LANES))


def _ssd_scan_kernel(x_ref, b_ref, c_ref, dt_ref, dtt_ref, a_ref, at_ref, y_ref, st_sc):
    d = pl.program_id(1)
    s = pl.program_id(3)
    Q = SSM_CHUNK
    P = SSM_HEADDIM

    @pl.when(s == 0)
    def _():
        st_sc[...] = jnp.zeros(st_sc.shape, F32)

    sgn = 1 - 2 * d
    ri = lax.broadcasted_iota(jnp.int32, (Q, Q), 0)
    ci = lax.broadcasted_iota(jnp.int32, (Q, Q), 1)
    mask = (ri - ci) * sgn >= 0
    tri = mask.astype(F32)
    tri_t = ((ci - ri) * sgn >= 0).astype(F32)

    dt = dt_ref[...]
    a = dt * a_ref[...]
    a_t = dtt_ref[...] * at_ref[...]
    hi = lax.Precision.HIGHEST
    acs = jnp.dot(tri, a, preferred_element_type=F32, precision=hi)
    acs_t = jnp.dot(a_t, tri_t, preferred_element_type=F32, precision=hi)
    tot = jnp.sum(a, axis=0, keepdims=True)

    bq = b_ref[...]
    cq = c_ref[...]
    cb = _qk(cq, bq)
    for hh in range(SSM_HG):
        xh = x_ref[:, hh * P:(hh + 1) * P].astype(F32)
        acs_c = acs[:, hh:hh + 1]
        dt_c = dt[:, hh:hh + 1]
        tot_h = tot[:, hh:hh + 1]
        seg = acs_c - acs_t[hh:hh + 1, :]
        decay = jnp.exp(jnp.where(mask, seg, -jnp.inf))
        m = (cb * decay).astype(BF16)
        y = jnp.dot(m, (xh * dt_c).astype(BF16), preferred_element_type=F32)
        st = st_sc[hh]
        y_off = _qk(cq, st.astype(BF16))
        y_ref[:, hh * P:(hh + 1) * P] = (y + y_off * jnp.exp(acs_c)).astype(y_ref.dtype)
        w = jnp.exp(tot_h - acs_c) * dt_c
        upd = lax.dot_general((xh * w).astype(BF16), bq, (((0,), (0,)), ((), ())),
                              preferred_element_type=F32)
        st_sc[hh] = jnp.exp(tot_h) * st + upd


def ssd_scan(xbc_act, dt5, dtt5, a_row, a_col, n_lat):
    B, S, _ = xbc_act.shape
    Q = SSM_CHUNK
    G, HG, P, N = SSM_GROUPS, SSM_HG, SSM_HEADDIM, SSM_STATE
    nch = S // Q
    latc = n_lat // Q
    ncc = nch - latc
    gw = HG * P
    bcol = SSM_D_INNER // N
    ccol = bcol + G

    def chunk(d, s):
        fwd = jnp.where(s < ncc, latc + s, s - ncc)
        return jnp.where(d == 0, fwd, nch - 1 - s)

    return pl.pallas_call(
        _ssd_scan_kernel,
        grid=(B, 2, G, nch),
        in_specs=[
            pl.BlockSpec((None, Q, gw), lambda b, d, g, s: (b, chunk(d, s), g)),
            pl.BlockSpec((None, Q, N), lambda b, d, g, s: (b, chunk(d, s), bcol + g)),
            pl.BlockSpec((None, Q, N), lambda b, d, g, s: (b, chunk(d, s), ccol + g)),
            pl.BlockSpec((None, None, None, Q, HG), lambda b, d, g, s: (b, d, g, chunk(d, s), 0)),
            pl.BlockSpec((None, None, None, HG, Q), lambda b, d, g, s: (b, d, g, 0, chunk(d, s))),
            pl.BlockSpec((None, None, 1, HG), lambda b, d, g, s: (d, g, 0, 0)),
            pl.BlockSpec((None, None, HG, 1), lambda b, d, g, s: (d, g, 0, 0)),
        ],
        out_specs=pl.BlockSpec((None, None, Q, gw), lambda b, d, g, s: (b, d, chunk(d, s), g)),
        out_shape=jax.ShapeDtypeStruct((B, 2, S, SSM_D_INNER), BF16),
        scratch_shapes=[pltpu.VMEM((HG, P, N), F32)],
        compiler_params=_cparams(("parallel", "parallel", "parallel", "arbitrary")),
        name="ssd_scan",
    )(xbc_act, xbc_act, xbc_act, dt5, dtt5, a_row, a_col)


def _ssd_finish_kernel(yf_ref, yb_ref, x_ref, z_ref, d_ref, g_ref, o_ref):
    y = yf_ref[...].astype(F32) + yb_ref[...].astype(F32) + d_ref[...] * x_ref[...].astype(F32)
    z = z_ref[...].astype(F32)
    y = y * (z * (1.0 / (1.0 + jnp.exp(-z))))
    o_ref[...] = _rms(y, g_ref[...]).astype(o_ref.dtype)


def ssd_finish(y2, xbc_act, z, d_full, norm_g):
    B, _, S, C = y2.shape
    tr = ROW_TILE
    row = pl.BlockSpec((None, tr, C), lambda b, t: (b, t, 0))
    vec = pl.BlockSpec((1, C), lambda b, t: (0, 0))
    return pl.pallas_call(
        _ssd_finish_kernel,
        grid=(B, S // tr),
        in_specs=[pl.BlockSpec((None, None, tr, C), lambda b, t: (b, 0, t, 0)),
                  pl.BlockSpec((None, None, tr, C), lambda b, t: (b, 1, t, 0)),
                  row, row, vec, vec],
        out_specs=row,
        out_shape=jax.ShapeDtypeStruct((B, S, C), BF16),
        compiler_params=_cparams(("parallel", "parallel")),
        name="ssd_finish",
    )(y2, y2, xbc_act, z, d_full.reshape(1, C), norm_g.reshape(1, C))


def _extract_topk(x, k):
    n = x.shape[0]
    iota = lax.broadcasted_iota(jnp.int32, x.shape, 0)
    work = x
    member = jnp.zeros(x.shape, F32)
    vals = []
    for _ in range(k):
        m = jnp.max(work, axis=0, keepdims=True)
        idx = jnp.min(jnp.where(work == m, iota, n), axis=0, keepdims=True)
        sel = iota == idx
        work = jnp.where(sel, -jnp.inf, work)
        member = jnp.where(sel, 1.0, member)
        vals.append(m)
    return jnp.concatenate(vals, axis=0), member


_STAIR = [(a, b) for a in range(PEER_TOPK) for b in range(PEER_TOPK) if (a + 1) * (b + 1) <= PEER_TOPK]


def _peer_route_kernel(q_ref, k1_ref, k2_ref, s1_ref, s2_ref, e1_ref, e2_ref, tau_ref):
    for hd in range(PEER_HEADS):
        q1 = q_ref[:, (2 * hd) * PEER_HALF:(2 * hd + 1) * PEER_HALF]
        q2 = q_ref[:, (2 * hd + 1) * PEER_HALF:(2 * hd + 2) * PEER_HALF]
        s1 = _qk(k1_ref[hd], q1)
        s2 = _qk(k2_ref[hd], q2)
        v1, m1 = _extract_topk(s1, PEER_TOPK)
        v2, m2 = _extract_topk(s2, PEER_TOPK)
        pad_rows = [jnp.full((-len(_STAIR) % 8, v1.shape[1]), -jnp.inf, F32)]
        cand = jnp.concatenate([v1[a:a + 1, :] + v2[b:b + 1, :] for a, b in _STAIR] + pad_rows, axis=0)
        top, sel = _extract_topk(cand, PEER_TOPK)
        tau = top[PEER_TOPK - 1:PEER_TOPK, :]
        cmax = top[0:1, :]
        z = jnp.sum(sel * jnp.exp(cand - cmax), axis=0, keepdims=True)
        e1 = m1 * jnp.exp(s1 - v1[0:1, :]) / z
        e2 = m2 * jnp.exp(s2 - v2[0:1, :])
        s1_ref[hd] = s1
        s2_ref[hd] = s2
        e1_ref[hd] = e1
        e2_ref[hd] = e2
        tau_ref[hd] = tau


def peer_route(q, k1, k2):
    M = q.shape[0]
    tm = _pick(M, (256, 128))
    H, K = PEER_HEADS, PEER_N_KEYS
    big = pl.BlockSpec((H, K, tm), lambda i: (0, 0, i))
    kspec = pl.BlockSpec((H, K, PEER_HALF), lambda i: (0, 0, 0))
    sd = jax.ShapeDtypeStruct((H, K, M), F32)
    return pl.pallas_call(
        _peer_route_kernel,
        grid=(M // tm,),
        in_specs=[pl.BlockSpec((tm, 2 * H * PEER_HALF), lambda i: (i, 0)), kspec, kspec],
        out_specs=[big, big, big, big, pl.BlockSpec((H, 1, tm), lambda i: (0, 0, i))],
        out_shape=[sd, sd, sd, sd, jax.ShapeDtypeStruct((H, 1, M), F32)],
        compiler_params=_cparams(("parallel",)),
        name="peer_route",
    )(q, k1, k2)


def _gelu_tanh(x):
    c = 0.7978845608028654
    t = jnp.tanh(x * (c + (c * 0.044715) * (x * x)))
    hx = 0.5 * x
    return hx + hx * t


PEER_TE = 512
PEER_K1_PER_STEP = 2 * PEER_TE // PEER_N_KEYS


def _peer_gates(act, k1_base, s1_ref, e1_ref, s2_ref, e2_ref, tau_ref, p_ref):
    for kk in range(PEER_TE // PEER_N_KEYS):
        rows = slice(kk * PEER_N_KEYS, (kk + 1) * PEER_N_KEYS)
        k1 = k1_base + kk
        g = jnp.zeros((PEER_N_KEYS, act.shape[1]), F32)
        for hd in range(PEER_HEADS):
            cand = s1_ref[hd, k1:k1 + 1, :] + s2_ref[hd]
            g = g + jnp.where(cand >= tau_ref[hd], e1_ref[hd, k1:k1 + 1, :] * e2_ref[hd], 0.0)
        p_ref[rows, :] = (g * _gelu_tanh(act[rows, :])).astype(BF16)


def _peer_dense_kernel(h_ref, u_ref, vtp_ref, vtc_ref, s1_ref, e1_ref, s2_ref, e2_ref, tau_ref, o_ref,
                       acc_sc, pa_sc, pb_sc, *, nsteps):
    j = pl.program_id(1)
    gate_refs = (s1_ref, e1_ref, s2_ref, e2_ref, tau_ref)

    @pl.when(j == 0)
    def _():
        acc_sc[...] = jnp.zeros(acc_sc.shape, F32)
        pb_sc[...] = jnp.zeros(pb_sc.shape, BF16)

    @pl.when(j < nsteps)
    def _():
        h = h_ref[...]
        act_a = _qk(u_ref[:PEER_TE, :], h)
        acc_sc[...] += jnp.dot(vtp_ref[...], pb_sc[...], preferred_element_type=F32)
        act_b = _qk(u_ref[PEER_TE:, :], h)
        _peer_gates(act_a, 0, *gate_refs, pa_sc)
        acc_sc[...] += jnp.dot(vtc_ref[...], pa_sc[...], preferred_element_type=F32)
        _peer_gates(act_b, PEER_TE // PEER_N_KEYS, *gate_refs, pb_sc)

    @pl.when(j == nsteps)
    def _():
        o_ref[...] = (acc_sc[...] + jnp.dot(vtp_ref[...], pb_sc[...], preferred_element_type=F32)).T


def peer_dense(h, u, vt, s1, s2, e1, e2, tau):
    M, D = h.shape
    E = u.shape[0]
    tm = _pick(M, (512, 384, 256, 128))
    te = PEER_TE
    ns = E // (2 * te)
    H, K = PEER_HEADS, PEER_N_KEYS
    kstep = PEER_K1_PER_STEP
    s1b = s1.reshape(H, K // kstep, kstep, M)
    e1b = e1.reshape(H, K // kstep, kstep, M)
    big = pl.BlockSpec((H, K, tm), lambda i, j: (0, 0, i))
    small = pl.BlockSpec((H, None, kstep, tm), lambda i, j: (0, jnp.minimum(j, ns - 1), 0, i))
    return pl.pallas_call(
        functools.partial(_peer_dense_kernel, nsteps=ns),
        grid=(M // tm, ns + 1),
        in_specs=[pl.BlockSpec((tm, D), lambda i, j: (i, 0)),
                  pl.BlockSpec((2 * te, D), lambda i, j: (jnp.minimum(j, ns - 1), 0)),
                  pl.BlockSpec((D, te), lambda i, j: (0, jnp.maximum(2 * j - 1, 0))),
                  pl.BlockSpec((D, te), lambda i, j: (0, jnp.minimum(2 * j, 2 * ns - 1))),
                  small, small, big, big,
                  pl.BlockSpec((H, 1, tm), lambda i, j: (0, 0, i))],
        out_specs=pl.BlockSpec((tm, D), lambda i, j: (i, 0)),
        out_shape=jax.ShapeDtypeStruct((M, D), F32),
        scratch_shapes=[pltpu.VMEM((D, tm), F32), pltpu.VMEM((te, tm), BF16), pltpu.VMEM((te, tm), BF16)],
        compiler_params=_cparams(("parallel", "arbitrary")),
        name="peer_dense",
    )(h, u, vt, vt, s1b, e1b, s2, e2, tau)


def peer_ffn(h, w_q, k1, k2, u, vt):
    q = matmul(h, w_q, BF16, "peer_q")
    s1, s2, e1, e2, tau = peer_route(q, k1, k2)
    return peer_dense(h, u, vt, s1, s2, e1, e2, tau)


def _mla_weights(w_a, w_uq, w_ukv):
    H, dn, dr, dv = MLA_HEADS, MLA_NOPE, MLA_ROPE, MLA_V
    pad_a = jnp.zeros((w_a.shape[0], LANES - dr), w_a.dtype)
    w_a_p = jnp.concatenate([w_a, pad_a], axis=1).astype(BF16)
    wq = w_uq.reshape(MLA_Q_RANK, H, dn + dr)
    wq_n = wq[:, :, :dn].reshape(MLA_Q_RANK, H * dn)
    wq_r = jnp.concatenate([wq[:, :, dn:], jnp.zeros((MLA_Q_RANK, H, LANES - dr), wq.dtype)], axis=2)
    w_q_p = jnp.concatenate([wq_n, wq_r.reshape(MLA_Q_RANK, H * LANES)], axis=1).astype(BF16)
    wkv = w_ukv.reshape(MLA_KV_RANK, H, dn + dv)
    w_kv_p = jnp.concatenate([wkv[:, :, :dn].reshape(MLA_KV_RANK, H * dn),
                              wkv[:, :, dn:].reshape(MLA_KV_RANK, H * dv)], axis=1).astype(BF16)
    return w_a_p, w_q_p, w_kv_p


def mla_mixer(h, w_a, q_g, kv_g, w_uq, w_ukv, w_o, tables, n_lat, n_ctx):
    B, S, D = h.shape
    M = B * S
    w_a_p, w_q_p, w_kv_p = _mla_weights(w_a, w_uq, w_ukv)
    ca = matmul(h.reshape(M, D), w_a_p, F32, "mla_a").reshape(B, S, -1)
    qc, kvc, krp = mla_prep(ca, q_g, kv_g, tables)
    q = matmul(qc.reshape(M, -1), w_q_p, BF16, "mla_q").reshape(B, S, -1)
    kv = matmul(kvc.reshape(M, -1), w_kv_p, BF16, "mla_kv").reshape(B, S, -1)
    o = mla_attention_all(q, kv, krp, tables, n_lat, n_ctx)
    return matmul(o.reshape(M, -1), w_o.astype(BF16), F32, "mla_o").reshape(B, S, D)


def ssd_mixer(h, w_in, conv_w, conv_b, a_log, dt_bias, d_skip, norm_g, w_out, n_lat):
    B, S, D = h.shape
    M = B * S
    G, HG = SSM_GROUPS, SSM_HG
    h2 = h.reshape(M, D)
    w_bf = w_in.astype(BF16)
    z = matmul(h2, w_bf[:, :SSM_D_INNER], BF16, "ssd_in_z").reshape(B, S, -1)
    xbc = matmul(h2, w_bf[:, SSM_D_INNER:SSM_D_INNER + SSM_CONV_DIM], BF16, "ssd_in_x").reshape(B, S, -1)
    dt_raw = matmul(h2, w_bf[:, SSM_D_INNER + SSM_CONV_DIM:], F32, "ssd_in_dt").reshape(B, S, -1)
    xbc_act, dt = ssd_conv(xbc, dt_raw, conv_w, conv_b, dt_bias.reshape(-1), n_lat)
    dt5 = dt.reshape(B, S, 2, G, HG).transpose(0, 2, 3, 1, 4)
    dtt5 = dt5.transpose(0, 1, 2, 4, 3)
    a = -jnp.exp(a_log.astype(F32)).reshape(2, G, HG)
    y2 = ssd_scan(xbc_act, dt5, dtt5, a[:, :, None, :], a[:, :, :, None], n_lat)
    d_full = jnp.repeat(d_skip.astype(F32), SSM_HEADDIM)
    yn = ssd_finish(y2, xbc_act, z, d_full, norm_g)
    return matmul(yn.reshape(M, -1), w_out.astype(BF16), F32, "ssd_out").reshape(B, S, D)


def window_mixer(h, w_qkv, sinks, w_o, tables, n_lat, n_ctx):
    B, S, D = h.shape
    M = B * S
    qkv = matmul(h.reshape(M, D), w_qkv.astype(BF16), BF16, "win_qkv").reshape(B, S, -1)
    qk = rope_cols(qkv, (WQ_HEADS + WKV_HEADS) * W_HEAD_DIM, tables)
    o = window_attention(qk, qkv, sinks.astype(F32), n_lat, n_ctx)
    return matmul(o.reshape(M, -1), w_o.astype(BF16), F32, "win_o").reshape(B, S, D)


def kernel(x, c, ctx, c_ctx, ada_w, ada_b, norm_mix_g, norm_ffn_g, mla_w_a, mla_q_norm_g, mla_kv_norm_g,
           mla_w_uq, mla_w_ukv, mla_w_o, ssd_w_in, ssd_conv_w, ssd_conv_b, ssd_a_log, ssd_dt_bias,
           ssd_d_skip, ssd_norm_g, ssd_w_out, win_w_qkv, win_sinks, win_w_o, peer_w_q, peer_k1, peer_k2,
           peer_u, peer_v, final_norm_g):
    B, L, D = x.shape
    Lc = ctx.shape[1]
    S = L + Lc
    M = B * S
    depth = ada_w.shape[0]
    xs = jnp.concatenate([x, ctx], axis=1)

    n_grp = 8
    cmat = jnp.zeros((n_grp, D), F32).at[:B].set(c).at[B].set(c_ctx)
    mod_all = adaln_all(cmat, ada_w, ada_b).reshape(depth, n_grp, ADA_CHUNKS, 1, D)

    tab_mla = rope_tables(L, Lc, repeat_upper=False)
    tab_win = rope_tables(L, Lc, repeat_upper=True)

    h = None
    ffn = None
    for i in range(depth):
        mod = mod_all[i]
        if i == 0:
            (h,) = resid_norm_mod(xs, None, None, None, norm_mix_g[i], mod, 0, 1, L)
        else:
            xs, h = resid_norm_mod(xs, ffn, mod_all[i - 1], 5, norm_mix_g[i], mod, 0, 1, L)
        kind, j = i % N_MIXERS, i // N_MIXERS
        if kind == 0:
            mix = mla_mixer(h, mla_w_a[j], mla_q_norm_g[j], mla_kv_norm_g[j], mla_w_uq[j], mla_w_ukv[j],
                            mla_w_o[j], tab_mla, L, Lc)
        elif kind == 1:
            mix = ssd_mixer(h, ssd_w_in[j], ssd_conv_w[j], ssd_conv_b[j], ssd_a_log[j], ssd_dt_bias[j],
                            ssd_d_skip[j], ssd_norm_g[j], ssd_w_out[j], L)
        else:
            mix = window_mixer(h, win_w_qkv[j], win_sinks[j], win_w_o[j], tab_win, L, Lc)
        xs, h2 = resid_norm_mod(xs, mix, mod, 2, norm_ffn_g[i], mod, 3, 4, L)
        ffn = peer_ffn(h2.reshape(M, D), peer_w_q[i].astype(BF16),
                       peer_k1[i].astype(BF16), peer_k2[i].astype(BF16),
                       peer_u[i].astype(BF16), peer_v[i].T.astype(BF16)).reshape(B, S, D)
    (out,) = resid_norm_mod(xs, ffn, mod_all[depth - 1], 5, final_norm_g, None, None, None, L, final=True)
    return out
```

```python
import functools
import math

import jax
import jax.numpy as jnp
from jax import lax
from jax.experimental import pallas as pl
from jax.experimental.pallas import tpu as pltpu

F32 = jnp.float32
BF16 = jnp.bfloat16

D_MODEL = 2048
DEPTH = 4
GRID_W = 64
N_MIXERS = 3
NORM_EPS = 1e-6
ROPE_THETA = 10000.0
NEG_INF = -1e30
ADA_CHUNKS = 6

MLA_HEADS = 16
MLA_Q_RANK = 512
MLA_KV_RANK = 512
MLA_NOPE = 128
MLA_ROPE = 64
MLA_V = 128
MLA_VT_ROWS = MLA_V + 16

SSM_D_INNER = 2 * D_MODEL
SSM_HEADDIM = 64
SSM_HEADS = SSM_D_INNER // SSM_HEADDIM
SSM_GROUPS = 8
SSM_HG = SSM_HEADS // SSM_GROUPS
SSM_STATE = 128
SSM_CONV = 5
SSM_CHUNK = 128
SSM_BC = SSM_GROUPS * SSM_STATE
SSM_CONV_DIM = SSM_D_INNER + 2 * SSM_BC

WINDOW = 128
WQ_HEADS = 32
WKV_HEADS = 8
W_HEAD_DIM = 64

PEER_HEADS = 8
PEER_N_KEYS = 128
PEER_N_EXPERTS = PEER_N_KEYS * PEER_N_KEYS
PEER_TOPK = 16
PEER_HALF = 128

LANES = 128
VMEM_LIMIT = 56 * 1024 * 1024
ROW_TILE = 256


def _pick(n, candidates):
    for c in candidates:
        if n % c == 0:
            return c
    raise ValueError(f"no tile for {n} in {candidates}")


def _cparams(sem):
    return pltpu.CompilerParams(dimension_semantics=sem, vmem_limit_bytes=VMEM_LIMIT)


def _mm_kernel(a_ref, w_ref, o_ref):
    o_ref[...] = jnp.dot(a_ref[...], w_ref[...], preferred_element_type=F32).astype(o_ref.dtype)


def matmul(a, w, out_dtype, name):
    M, K = a.shape
    N = w.shape[1]
    tm = _pick(M, (1056, 1024, 768, 512, 384, 256, 128, 64, 32, 16))
    tn = _pick(N, (512, 384, 256, 128))
    return pl.pallas_call(
        _mm_kernel,
        grid=(M // tm, N // tn),
        in_specs=[pl.BlockSpec((tm, K), lambda i, j: (i, 0)),
                  pl.BlockSpec((K, tn), lambda i, j: (0, j))],
        out_specs=pl.BlockSpec((tm, tn), lambda i, j: (i, j)),
        out_shape=jax.ShapeDtypeStruct((M, N), out_dtype),
        compiler_params=_cparams(("parallel", "parallel")),
        name=name,
    )(a, w)


def _adaln_kernel(c_ref, w_ref, b_ref, o_ref):
    c = c_ref[...]
    s = c * (1.0 / (1.0 + jnp.exp(-c)))
    o_ref[...] = jnp.dot(s, w_ref[...], preferred_element_type=F32,
                         precision=lax.Precision.HIGHEST) + b_ref[...]


def adaln_all(cmat, ada_w, ada_b):
    depth, d, n = ada_w.shape
    rows = cmat.shape[0]
    tn = _pick(n, (1024, 512, 256, 128))
    return pl.pallas_call(
        _adaln_kernel,
        grid=(depth, n // tn),
        in_specs=[pl.BlockSpec((rows, d), lambda l, j: (0, 0)),
                  pl.BlockSpec((None, d, tn), lambda l, j: (l, 0, j)),
                  pl.BlockSpec((None, 1, tn), lambda l, j: (l, 0, j))],
        out_specs=pl.BlockSpec((None, rows, tn), lambda l, j: (l, 0, j)),
        out_shape=jax.ShapeDtypeStruct((depth, rows, n), F32),
        compiler_params=_cparams(("parallel", "parallel")),
        name="adaln",
    )(cmat, ada_w, ada_b.reshape(depth, 1, n))


def _rms(x, g):
    var = jnp.mean(x * x, axis=-1, keepdims=True)
    return x * lax.rsqrt(var + NORM_EPS) * g


def _rnm_kernel(*refs, has_mix, modulate):
    it = iter(refs)
    x_ref = next(it)
    if has_mix:
        mix_ref, gate_ref = next(it), next(it)
    g_ref = next(it)
    if modulate:
        shift_ref, scale_ref = next(it), next(it)
    if has_mix and modulate:
        xo_ref = next(it)
    h_ref = next(it)
    x = x_ref[...]
    if has_mix:
        x = x + gate_ref[...] * mix_ref[...].astype(F32)
        if modulate:
            xo_ref[...] = x
    y = _rms(x, g_ref[...])
    if modulate:
        y = y * (1.0 + scale_ref[...]) + shift_ref[...]
    h_ref[...] = y.astype(h_ref.dtype)


def resid_norm_mod(x, mix, gate_mod, gate_k, norm_g, mod, shift_k, scale_k, n_lat, *, final=False):
    B, S, D = x.shape
    tr = ROW_TILE
    lat_tiles = n_lat // tr
    tiles = lat_tiles if final else S // tr
    has_mix = mix is not None
    modulate = not final

    def grp(b, t):
        return jnp.where(t < lat_tiles, b, B)

    row = pl.BlockSpec((None, tr, D), lambda b, t: (b, t, 0))

    def modspec(k):
        return pl.BlockSpec((None, None, 1, D), lambda b, t: (grp(b, t), k, 0, 0))

    args, specs = [x], [row]
    if has_mix:
        args += [mix, gate_mod]
        specs += [row, modspec(gate_k)]
    args.append(norm_g.reshape(1, D))
    specs.append(pl.BlockSpec((1, D), lambda b, t: (0, 0)))
    if modulate:
        args += [mod, mod]
        specs += [modspec(shift_k), modspec(scale_k)]
    out_shapes, out_specs = [], []
    if has_mix and modulate:
        out_shapes.append(jax.ShapeDtypeStruct((B, S, D), F32))
        out_specs.append(row)
    if final:
        out_shapes.append(jax.ShapeDtypeStruct((B, n_lat, D), F32))
    else:
        out_shapes.append(jax.ShapeDtypeStruct((B, S, D), BF16))
    out_specs.append(row)
    outs = pl.pallas_call(
        functools.partial(_rnm_kernel, has_mix=has_mix, modulate=modulate),
        grid=(B, tiles),
        in_specs=specs,
        out_specs=out_specs,
        out_shape=out_shapes,
        compiler_params=_cparams(("parallel", "parallel")),
        name="resid_norm_mod",
    )(*args)
    return outs


def rope_tables(n_lat, n_ctx, repeat_upper):
    half = MLA_ROPE // 4
    pos = jnp.arange(n_lat, dtype=jnp.int32)
    rows = (pos // GRID_W).astype(F32)
    cols = (pos % GRID_W).astype(F32)
    freqs = ROPE_THETA ** (-jnp.arange(half, dtype=F32) / half)
    ang_r = rows[:, None] * freqs
    ang_c = cols[:, None] * freqs
    ang = jnp.concatenate([ang_r, ang_r, ang_c, ang_c], axis=-1)
    cos, sin = jnp.cos(ang), jnp.sin(ang)
    first = (jnp.arange(64) % 32) < 16
    sa = jnp.where(first, -sin, 0.0)
    sb = jnp.where(first, 0.0, sin)
    cos = jnp.concatenate([cos, jnp.ones((n_ctx, 64), F32)], axis=0)
    sa = jnp.concatenate([sa, jnp.zeros((n_ctx, 64), F32)], axis=0)
    sb = jnp.concatenate([sb, jnp.zeros((n_ctx, 64), F32)], axis=0)
    if repeat_upper:
        return tuple(jnp.concatenate([t, t], axis=-1) for t in (cos, sa, sb))
    z = jnp.zeros_like(cos)
    return tuple(jnp.concatenate([t, z], axis=-1) for t in (cos, sa, sb))


def _rope128(x, cos, sa, sb):
    return x * cos + pltpu.roll(x, 112, 1) * sa + pltpu.roll(x, 16, 1) * sb


def _rope_cols_kernel(x_ref, cos_ref, sa_ref, sb_ref, o_ref):
    cos, sa, sb = cos_ref[...], sa_ref[...], sb_ref[...]
    for c in range(x_ref.shape[-1] // LANES):
        sl = slice(c * LANES, (c + 1) * LANES)
        o_ref[:, sl] = _rope128(x_ref[:, sl].astype(F32), cos, sa, sb).astype(o_ref.dtype)


def rope_cols(x, width, tables):
    B, S, _ = x.shape
    tr = ROW_TILE
    tab = pl.BlockSpec((tr, LANES), lambda b, t: (t, 0))
    return pl.pallas_call(
        _rope_cols_kernel,
        grid=(B, S // tr),
        in_specs=[pl.BlockSpec((None, tr, width), lambda b, t: (b, t, 0)), tab, tab, tab],
        out_specs=pl.BlockSpec((None, tr, width), lambda b, t: (b, t, 0)),
        out_shape=jax.ShapeDtypeStruct((B, S, width), BF16),
        compiler_params=_cparams(("parallel", "parallel")),
        name="rope_cols",
    )(x, *tables)


def _mla_prep_kernel(ca_ref, qg_ref, kvg_ref, cos_ref, sa_ref, sb_ref, qc_ref, kvc_ref, kr_ref):
    cq = ca_ref[:, :MLA_Q_RANK]
    ckv = ca_ref[:, MLA_Q_RANK:MLA_Q_RANK + MLA_KV_RANK]
    kr = ca_ref[:, MLA_Q_RANK + MLA_KV_RANK:]
    qc_ref[...] = _rms(cq, qg_ref[...]).astype(BF16)
    kvc_ref[...] = _rms(ckv, kvg_ref[...]).astype(BF16)
    kr_ref[...] = _rope128(kr, cos_ref[...], sa_ref[...], sb_ref[...]).astype(BF16)


def mla_prep(ca, q_g, kv_g, tables):
    B, S, C = ca.shape
    tr = ROW_TILE
    tab = pl.BlockSpec((tr, LANES), lambda b, t: (t, 0))

    def row(w):
        return pl.BlockSpec((None, tr, w), lambda b, t: (b, t, 0))

    def vec(w):
        return pl.BlockSpec((1, w), lambda b, t: (0, 0))

    return pl.pallas_call(
        _mla_prep_kernel,
        grid=(B, S // tr),
        in_specs=[row(C), vec(MLA_Q_RANK), vec(MLA_KV_RANK), tab, tab, tab],
        out_specs=[row(MLA_Q_RANK), row(MLA_KV_RANK), row(LANES)],
        out_shape=[jax.ShapeDtypeStruct((B, S, MLA_Q_RANK), BF16),
                   jax.ShapeDtypeStruct((B, S, MLA_KV_RANK), BF16),
                   jax.ShapeDtypeStruct((B, S, LANES), BF16)],
        compiler_params=_cparams(("parallel", "parallel")),
        name="mla_prep",
    )(ca, q_g.reshape(1, -1), kv_g.reshape(1, -1), *tables)


def _mla_attn_kernel(qn_ref, qr_ref, cos_ref, sa_ref, sb_ref, kn_ref, kr_ref, vt_ref, o_ref,
                     q_sc, m_sc, acc_sc, *, qscale, nk, nsub):
    j = pl.program_id(3)
    tqs = q_sc.shape[0] // nsub

    @pl.when(j == 0)
    def _():
        qr = _rope128(qr_ref[...].astype(F32), cos_ref[...], sa_ref[...], sb_ref[...])
        q_sc[:, :LANES] = (qn_ref[...].astype(F32) * qscale).astype(BF16)
        q_sc[:, LANES:] = (qr * qscale).astype(BF16)
        m_sc[...] = jnp.full(m_sc.shape, -jnp.inf, F32)
        acc_sc[...] = jnp.zeros(acc_sc.shape, F32)

    k = jnp.concatenate([kn_ref[...], kr_ref[...]], axis=1)
    scores = [_qk(k, q_sc[u * tqs:(u + 1) * tqs, :]) for u in range(nsub)]
    for u in range(nsub):
        cols = slice(u * tqs, (u + 1) * tqs)
        s = scores[u]
        m_prev = m_sc[:, cols]
        m_new = jnp.maximum(m_prev, jnp.max(s, axis=0, keepdims=True))
        alpha = jnp.exp2(m_prev - m_new)
        p = jnp.exp2((s - m_new).astype(BF16))
        acc_sc[:, cols] = alpha * acc_sc[:, cols] + jnp.dot(vt_ref[...], p, preferred_element_type=F32)
        m_sc[:, cols] = m_new

    @pl.when(j == nk - 1)
    def _():
        o_ref[...] = (acc_sc[:MLA_V, :] / acc_sc[MLA_V:MLA_V + 1, :]).T.astype(o_ref.dtype)


def _qk(q, k):
    return lax.dot_general(q, k, (((1,), (1,)), ((), ())), preferred_element_type=F32)


def mla_attention(q, kv, vt, krp, tables, o_prev, *, q_start, q_len, k_start, k_len):
    B, S, _ = q.shape
    H = MLA_HEADS
    tq = _pick(q_len, (2048, 1024, 512, 256))
    nsub = max(tq // 512, 1)
    tk = _pick(k_len, (768, 512, 256))
    assert q_start % tq == 0 and k_start % tk == 0
    qo, ko = q_start // tq, k_start // tk
    nq, nk = q_len // tq, k_len // tk
    qscale = (MLA_NOPE + MLA_ROPE) ** -0.5 * math.log2(math.e)
    tab = pl.BlockSpec((tq, LANES), lambda b, h, i, j: (i + qo, 0))
    in_specs = [
        pl.BlockSpec((None, tq, LANES), lambda b, h, i, j: (b, i + qo, h)),
        pl.BlockSpec((None, tq, LANES), lambda b, h, i, j: (b, i + qo, H + h)),
        tab, tab, tab,
        pl.BlockSpec((None, tk, LANES), lambda b, h, i, j: (b, j + ko, h)),
        pl.BlockSpec((None, tk, LANES), lambda b, h, i, j: (b, j + ko, 0)),
        pl.BlockSpec((None, None, MLA_VT_ROWS, tk), lambda b, h, i, j: (b, h, 0, j + ko)),
    ]
    args = [q, q, *tables, kv, krp, vt]
    aliases = {}
    if o_prev is not None:
        in_specs.append(pl.BlockSpec(memory_space=pl.ANY))
        args.append(o_prev)
        aliases = {len(args) - 1: 0}

    def body(*refs):
        if o_prev is not None:
            refs = refs[:8] + refs[9:]
        _mla_attn_kernel(*refs, qscale=qscale, nk=nk, nsub=nsub)

    return pl.pallas_call(
        body,
        grid=(B, H, nq, nk),
        in_specs=in_specs,
        out_specs=pl.BlockSpec((None, tq, LANES), lambda b, h, i, j: (b, i + qo, h)),
        out_shape=jax.ShapeDtypeStruct((B, S, H * MLA_V), BF16),
        scratch_shapes=[pltpu.VMEM((tq, 2 * LANES), BF16), pltpu.VMEM((1, tq), F32),
                        pltpu.VMEM((MLA_VT_ROWS, tq), F32)],
        input_output_aliases=aliases,
        compiler_params=_cparams(("parallel", "parallel", "parallel", "arbitrary")),
        name="mla_attn",
    )(*args)


def mla_attention_all(q, kv, krp, tables, n_lat, n_ctx):
    B, S, _ = q.shape
    H = MLA_HEADS
    vt = jnp.swapaxes(kv[:, :, H * MLA_NOPE:], 1, 2).reshape(B, H, MLA_V, S)
    vt = jnp.concatenate([vt, jnp.ones((B, H, MLA_VT_ROWS - MLA_V, S), BF16)], axis=2)
    o = mla_attention(q, kv, vt, krp, tables, None, q_start=0, q_len=n_lat, k_start=0, k_len=S)
    return mla_attention(q, kv, vt, krp, tables, o, q_start=n_lat, q_len=n_ctx, k_start=n_lat, k_len=n_ctx)


def _softmax_pv(s_list, v_list, sink):
    m = sink
    for s in s_list:
        m = jnp.maximum(m, jnp.max(s, axis=1, keepdims=True))
    den = jnp.exp(sink - m)
    o = None
    for s, v in zip(s_list, v_list):
        p = jnp.exp(s - m)
        den = den + jnp.sum(p, axis=1, keepdims=True)
        pv = jnp.dot(p.astype(BF16), v, preferred_element_type=F32)
        o = pv if o is None else o + pv
    return o / den


def _win_attn_kernel(sink_ref, q_ref, kp_ref, kc_ref, kn_ref, kx_ref, vp_ref, vc_ref, vn_ref, vx_ref,
                     o_ref, *, scale, nblk):
    p_idx = pl.program_id(1)
    i = pl.program_id(2)
    r = lax.broadcasted_iota(jnp.int32, (WINDOW, WINDOW), 0)
    c = lax.broadcasted_iota(jnp.int32, (WINDOW, WINDOW), 1)
    prev_ok = jnp.logical_and(c >= r, i > 0)
    next_ok = jnp.logical_and(c <= r, i < nblk - 1)
    hd = W_HEAD_DIM
    for hh in range(8):
        kvh = hh // 4
        ks = slice(kvh * hd, (kvh + 1) * hd)
        q = q_ref[:, hh * hd:(hh + 1) * hd]
        sink = jnp.full((1, 1), sink_ref[p_idx * 8 + hh], F32)
        s_p = jnp.where(prev_ok, _qk(q, kp_ref[:, ks]) * scale, NEG_INF)
        s_c = _qk(q, kc_ref[:, ks]) * scale
        s_n = jnp.where(next_ok, _qk(q, kn_ref[:, ks]) * scale, NEG_INF)
        s_x = _qk(q, kx_ref[:, ks]) * scale
        o = _softmax_pv([s_p, s_c, s_n, s_x],
                        [vp_ref[:, ks], vc_ref[:, ks], vn_ref[:, ks], vx_ref[:, ks]], sink)
        o_ref[:, hh * hd:(hh + 1) * hd] = o.astype(o_ref.dtype)


def _ctx_attn_kernel(sink_ref, q_ref, kx_ref, vx_ref, o_ref, *, scale):
    p_idx = pl.program_id(1)
    hd = W_HEAD_DIM
    for hh in range(8):
        kvh = hh // 4
        ks = slice(kvh * hd, (kvh + 1) * hd)
        q = q_ref[:, hh * hd:(hh + 1) * hd]
        sink = jnp.full((1, 1), sink_ref[p_idx * 8 + hh], F32)
        s_x = _qk(q, kx_ref[:, ks]) * scale
        o = _softmax_pv([s_x], [vx_ref[:, ks]], sink)
        o_ref[:, hh * hd:(hh + 1) * hd] = o.astype(o_ref.dtype)


def window_attention(qk, qkv, sinks, n_lat, n_ctx):
    B, S, _ = qkv.shape
    W = WINDOW
    nblk = n_lat // W
    scale = W_HEAD_DIM ** -0.5
    kcol = WQ_HEADS * W_HEAD_DIM // LANES
    vcol = (WQ_HEADS + WKV_HEADS) * W_HEAD_DIM // LANES
    assert n_lat % n_ctx == 0
    xblk = n_lat // n_ctx
    qw = 8 * W_HEAD_DIM
    smem = pl.BlockSpec(memory_space=pltpu.SMEM)

    def kspec(col, shift):
        return pl.BlockSpec((None, W, LANES),
                            lambda b, p, i: (b, jnp.clip(i + shift, 0, nblk - 1), col + p))

    def xspec(col):
        return pl.BlockSpec((None, n_ctx, LANES), lambda b, p, i: (b, xblk, col + p))

    o_lat = pl.pallas_call(
        functools.partial(_win_attn_kernel, scale=scale, nblk=nblk),
        grid=(B, WKV_HEADS // 2, nblk),
        in_specs=[smem, pl.BlockSpec((None, W, qw), lambda b, p, i: (b, i, p)),
                  kspec(kcol, -1), kspec(kcol, 0), kspec(kcol, 1), xspec(kcol),
                  kspec(vcol, -1), kspec(vcol, 0), kspec(vcol, 1), xspec(vcol)],
        out_specs=pl.BlockSpec((None, W, qw), lambda b, p, i: (b, i, p)),
        out_shape=jax.ShapeDtypeStruct((B, S, WQ_HEADS * W_HEAD_DIM), BF16),
        compiler_params=_cparams(("parallel", "parallel", "parallel")),
        name="win_attn",
    )(sinks, qk, qk, qk, qk, qk, qkv, qkv, qkv, qkv)

    def ctx_body(sink_ref, q_ref, kx_ref, vx_ref, _, o_ref):
        _ctx_attn_kernel(sink_ref, q_ref, kx_ref, vx_ref, o_ref, scale=scale)

    return pl.pallas_call(
        ctx_body,
        grid=(B, WKV_HEADS // 2, 1),
        in_specs=[smem, pl.BlockSpec((None, n_ctx, qw), lambda b, p, i: (b, xblk, p)),
                  xspec(kcol), xspec(vcol), pl.BlockSpec(memory_space=pl.ANY)],
        out_specs=pl.BlockSpec((None, n_ctx, qw), lambda b, p, i: (b, xblk, p)),
        out_shape=jax.ShapeDtypeStruct((B, S, WQ_HEADS * W_HEAD_DIM), BF16),
        input_output_aliases={4: 0},
        compiler_params=_cparams(("parallel", "parallel", "parallel")),
        name="win_attn_ctx",
    )(sinks, qk, qk, qkv, o_lat)


HALO = 16


def _conv_kernel(x_ref, xp_ref, xn_ref, w_ref, b_ref, dtr_ref, dtb_ref, o_ref, dt_ref, ext_sc,
                 *, lat_tiles, tiles):
    t = pl.program_id(1)
    tr = x_ref.shape[0]
    at_start = jnp.logical_or(t == 0, t == lat_tiles)
    at_end = jnp.logical_or(t == lat_tiles - 1, t == tiles - 1)
    ext_sc[0:HALO, :] = jnp.where(at_start, 0.0, xp_ref[...].astype(F32))
    ext_sc[HALO:HALO + tr, :] = x_ref[...].astype(F32)
    ext_sc[HALO + tr:, :] = jnp.where(at_end, 0.0, xn_ref[...].astype(F32))
    pad = (SSM_CONV - 1) // 2
    acc = jnp.broadcast_to(b_ref[...], (tr, x_ref.shape[1]))
    for k in range(SSM_CONV):
        acc = acc + w_ref[k:k + 1, :] * ext_sc[HALO - pad + k:HALO - pad + k + tr, :]
    o_ref[...] = (acc * (1.0 / (1.0 + jnp.exp(-acc)))).astype(o_ref.dtype)
    z = dtr_ref[...] + dtb_ref[...]
    dt_ref[...] = jnp.maximum(z, 0.0) + jnp.log(1.0 + jnp.exp(-jnp.abs(z)))


def ssd_conv(xbc, dt_raw, conv_w, conv_b, dt_bias, n_lat):
    B, S, C = xbc.shape
    tr = ROW_TILE
    lat_tiles, tiles = n_lat // tr, S // tr
    hb = tr // HALO
    nh = S // HALO
    row = pl.BlockSpec((None, tr, C), lambda b, t: (b, t, 0))
    dtrow = pl.BlockSpec((None, tr, LANES), lambda b, t: (b, t, 0))
    return pl.pallas_call(
        functools.partial(_conv_kernel, lat_tiles=lat_tiles, tiles=tiles),
        grid=(B, tiles),
        in_specs=[row,
                  pl.BlockSpec((None, HALO, C), lambda b, t: (b, jnp.maximum(t * hb - 1, 0), 0)),
                  pl.BlockSpec((None, HALO, C), lambda b, t: (b, jnp.minimum((t + 1) * hb, nh - 1), 0)),
                  pl.BlockSpec((SSM_CONV, C), lambda b, t: (0, 0)),
                  pl.BlockSpec((1, C), lambda b, t: (0, 0)),
                  dtrow,
                  pl.BlockSpec((1, LANES), lambda b, t: (0, 0))],
        out_specs=[row, dtrow],
        out_shape=[jax.ShapeDtypeStruct((B, S, C), BF16), jax.ShapeDtypeStruct((B, S, LANES), F32)],
        scratch_shapes=[pltpu.VMEM((tr + 2 * HALO, C), F32)],
        compiler_params=_cparams(("parallel", "parallel")),
        name="ssd_conv",
    )(xbc, xbc, xbc, conv_w, conv_b.reshape(1, C), dt_raw, dt_bias.reshape(1, LANES))


SSD_GROUPS_PER_STEP = 4


def _ssd_scan_kernel(x_ref, xt_ref, b_ref, c_ref, dt_ref, dtt_ref, a_ref, at_ref, y_ref, st_sc):
    d = pl.program_id(1)
    s = pl.program_id(3)
    Q = SSM_CHUNK
    P = SSM_HEADDIM

    @pl.when(s == 0)
    def _():
        st_sc[...] = jnp.zeros(st_sc.shape, F32)

    sgn = 1 - 2 * d
    ri = lax.broadcasted_iota(jnp.int32, (Q, Q), 0)
    ci = lax.broadcasted_iota(jnp.int32, (Q, Q), 1)
    mask = (ri - ci) * sgn >= 0
    tri = mask.astype(F32)
    tri_t = ((ci - ri) * sgn >= 0).astype(F32)

    hi = lax.Precision.HIGHEST
    N = SSM_STATE
    for gg in range(SSD_GROUPS_PER_STEP):
        dt = dt_ref[gg]
        dtt = dtt_ref[gg]
        a = dt * a_ref[gg]
        a_t = dtt * at_ref[gg]
        acs = jnp.dot(tri, a, preferred_element_type=F32, precision=hi)
        acs_t = jnp.dot(a_t, tri_t, preferred_element_type=F32, precision=hi)
        tot_c = jnp.sum(a_t, axis=1, keepdims=True)
        w_t = jnp.exp(tot_c - acs_t) * dtt
        e_tot = jnp.exp(tot_c)

        bq = b_ref[:, gg * N:(gg + 1) * N]
        cq = c_ref[:, gg * N:(gg + 1) * N]
        cb = _qk(cq, bq)
        for hh in range(SSM_HG):
            hx = gg * SSM_HG + hh
            cols = slice(hx * P, (hx + 1) * P)
            acs_c = acs[:, hh:hh + 1]
            seg = acs_c - acs_t[hh:hh + 1, :]
            decay = jnp.exp(jnp.where(mask, seg, -jnp.inf))
            m = (cb * decay * dtt[hh:hh + 1, :]).astype(BF16)
            y = jnp.dot(m, x_ref[:, cols], preferred_element_type=F32)
            st = st_sc[hx]
            y_off = _qk(cq, st.astype(BF16))
            y_ref[:, cols] = (y + y_off * jnp.exp(acs_c)).astype(y_ref.dtype)
            xw = (xt_ref[cols, :].astype(F32) * w_t[hh:hh + 1, :]).astype(BF16)
            upd = jnp.dot(xw, bq, preferred_element_type=F32)
            st_sc[hx] = e_tot[hh:hh + 1, :] * st + upd


def ssd_scan(xbc_act, xt, dt5, dtt5, a_row, a_col, n_lat):
    B, S, _ = xbc_act.shape
    Q = SSM_CHUNK
    G, HG, P, N = SSM_GROUPS, SSM_HG, SSM_HEADDIM, SSM_STATE
    gs = SSD_GROUPS_PER_STEP
    nch = S // Q
    latc = n_lat // Q
    ncc = nch - latc
    gw = gs * HG * P
    bcol = SSM_D_INNER // (gs * N)
    ccol = bcol + G // gs

    def chunk(d, s):
        fwd = jnp.where(s < ncc, latc + s, s - ncc)
        return jnp.where(d == 0, fwd, nch - 1 - s)

    return pl.pallas_call(
        _ssd_scan_kernel,
        grid=(B, 2, G // gs, nch),
        in_specs=[
            pl.BlockSpec((None, Q, gw), lambda b, d, g, s: (b, chunk(d, s), g)),
            pl.BlockSpec((None, gw, Q), lambda b, d, g, s: (b, g, chunk(d, s))),
            pl.BlockSpec((None, Q, gs * N), lambda b, d, g, s: (b, chunk(d, s), bcol + g)),
            pl.BlockSpec((None, Q, gs * N), lambda b, d, g, s: (b, chunk(d, s), ccol + g)),
            pl.BlockSpec((None, None, gs, Q, HG), lambda b, d, g, s: (b, d, g, chunk(d, s), 0)),
            pl.BlockSpec((None, None, gs, HG, Q), lambda b, d, g, s: (b, d, g, 0, chunk(d, s))),
            pl.BlockSpec((None, gs, 1, HG), lambda b, d, g, s: (d, g, 0, 0)),
            pl.BlockSpec((None, gs, HG, 1), lambda b, d, g, s: (d, g, 0, 0)),
        ],
        out_specs=pl.BlockSpec((None, None, Q, gw), lambda b, d, g, s: (b, d, chunk(d, s), g)),
        out_shape=jax.ShapeDtypeStruct((B, 2, S, SSM_D_INNER), BF16),
        scratch_shapes=[pltpu.VMEM((gs * HG, P, N), F32)],
        compiler_params=_cparams(("parallel", "parallel", "parallel", "arbitrary")),
        name="ssd_scan",
    )(xbc_act, xt, xbc_act, xbc_act, dt5, dtt5, a_row, a_col)


def _ssd_finish_kernel(yf_ref, yb_ref, x_ref, z_ref, d_ref, g_ref, o_ref):
    y = yf_ref[...].astype(F32) + yb_ref[...].astype(F32) + d_ref[...] * x_ref[...].astype(F32)
    z = z_ref[...].astype(F32)
    y = y * (z * (1.0 / (1.0 + jnp.exp(-z))))
    o_ref[...] = _rms(y, g_ref[...]).astype(o_ref.dtype)


def ssd_finish(y2, xbc_act, z, d_full, norm_g):
    B, _, S, C = y2.shape
    tr = ROW_TILE
    row = pl.BlockSpec((None, tr, C), lambda b, t: (b, t, 0))
    vec = pl.BlockSpec((1, C), lambda b, t: (0, 0))
    return pl.pallas_call(
        _ssd_finish_kernel,
        grid=(B, S // tr),
        in_specs=[pl.BlockSpec((None, None, tr, C), lambda b, t: (b, 0, t, 0)),
                  pl.BlockSpec((None, None, tr, C), lambda b, t: (b, 1, t, 0)),
                  row, row, vec, vec],
        out_specs=row,
        out_shape=jax.ShapeDtypeStruct((B, S, C), BF16),
        compiler_params=_cparams(("parallel", "parallel")),
        name="ssd_finish",
    )(y2, y2, xbc_act, z, d_full.reshape(1, C), norm_g.reshape(1, C))


def _extract_topk(x, k):
    n = x.shape[0]
    iota = lax.broadcasted_iota(jnp.int32, x.shape, 0)
    work = x
    member = jnp.zeros(x.shape, F32)
    vals = []
    for _ in range(k):
        m = jnp.max(work, axis=0, keepdims=True)
        idx = jnp.min(jnp.where(work == m, iota, n), axis=0, keepdims=True)
        sel = iota == idx
        work = jnp.where(sel, -jnp.inf, work)
        member = jnp.where(sel, 1.0, member)
        vals.append(m)
    return jnp.concatenate(vals, axis=0), member, jnp.zeros((1, x.shape[1]), F32)


def _extract_topk_distinct(x, k):
    work = x
    vals = []
    for _ in range(k):
        m = jnp.max(work, axis=0, keepdims=True)
        work = jnp.where(work == m, -jnp.inf, work)
        vals.append(m)
    member = jnp.where(work != x, 1.0, 0.0)
    miss = jnp.abs(jnp.sum(member, axis=0, keepdims=True) - float(k))
    return jnp.concatenate(vals, axis=0), member, miss


_STAIR = [(a, b) for a in range(PEER_TOPK) for b in range(PEER_TOPK) if (a + 1) * (b + 1) <= PEER_TOPK]


def _route_head(s1, s2, extract):
    v1, m1, miss1 = extract(s1, PEER_TOPK)
    v2, m2, miss2 = extract(s2, PEER_TOPK)
    pad_rows = [jnp.full((-len(_STAIR) % 8, v1.shape[1]), -jnp.inf, F32)]
    cand = jnp.concatenate([v1[a:a + 1, :] + v2[b:b + 1, :] for a, b in _STAIR] + pad_rows, axis=0)
    top, sel, miss3 = extract(cand, PEER_TOPK)
    tau = top[PEER_TOPK - 1:PEER_TOPK, :]
    cmax = top[0:1, :]
    z = jnp.sum(sel * jnp.exp(cand - cmax), axis=0, keepdims=True)
    e1 = m1 * jnp.exp(s1 - v1[0:1, :]) / z
    e2 = m2 * jnp.exp(s2 - v2[0:1, :])
    return e1, e2, tau, miss1 + miss2 + miss3


def _peer_route_kernel(q_ref, k1_ref, k2_ref, s1_ref, s2_ref, e1_ref, e2_ref, tau_ref):
    misses = []
    for hd in range(PEER_HEADS):
        q1 = q_ref[:, (2 * hd) * PEER_HALF:(2 * hd + 1) * PEER_HALF]
        q2 = q_ref[:, (2 * hd + 1) * PEER_HALF:(2 * hd + 2) * PEER_HALF]
        s1 = _qk(k1_ref[hd], q1)
        s2 = _qk(k2_ref[hd], q2)
        s1_ref[hd] = s1
        s2_ref[hd] = s2
        e1, e2, tau, miss = _route_head(s1, s2, _extract_topk_distinct)
        e1_ref[hd] = e1
        e2_ref[hd] = e2
        tau_ref[hd] = tau
        misses.append(jnp.max(miss))

    for hd in range(PEER_HEADS):
        @pl.when(misses[hd] > 0.0)
        def _():
            e1x, e2x, taux, _ = _route_head(s1_ref[hd], s2_ref[hd], _extract_topk)
            e1_ref[hd] = e1x
            e2_ref[hd] = e2x
            tau_ref[hd] = taux


def peer_route(q, k1, k2):
    M = q.shape[0]
    tm = _pick(M, (256, 128))
    H, K = PEER_HEADS, PEER_N_KEYS
    big = pl.BlockSpec((H, K, tm), lambda i: (0, 0, i))
    kspec = pl.BlockSpec((H, K, PEER_HALF), lambda i: (0, 0, 0))
    sd = jax.ShapeDtypeStruct((H, K, M), F32)
    return pl.pallas_call(
        _peer_route_kernel,
        grid=(M // tm,),
        in_specs=[pl.BlockSpec((tm, 2 * H * PEER_HALF), lambda i: (i, 0)), kspec, kspec],
        out_specs=[big, big, big, big, pl.BlockSpec((H, 1, tm), lambda i: (0, 0, i))],
        out_shape=[sd, sd, sd, sd, jax.ShapeDtypeStruct((H, 1, M), F32)],
        compiler_params=_cparams(("parallel",)),
        name="peer_route",
    )(q, k1, k2)


def _gelu_tanh(x):
    c = 0.7978845608028654
    t = jnp.tanh(x * (c + (c * 0.044715) * (x * x)))
    hx = 0.5 * x
    return hx + hx * t


PEER_TE = 512
PEER_K1_PER_STEP = 2 * PEER_TE // PEER_N_KEYS


def _peer_gates(act, k1_base, s1_ref, e1_ref, s2_ref, e2_ref, tau_ref, p_ref):
    for kk in range(PEER_TE // PEER_N_KEYS):
        rows = slice(kk * PEER_N_KEYS, (kk + 1) * PEER_N_KEYS)
        k1 = k1_base + kk
        g = jnp.zeros((PEER_N_KEYS, act.shape[1]), F32)
        for hd in range(PEER_HEADS):
            cand = s1_ref[hd, k1:k1 + 1, :] + s2_ref[hd]
            g = g + jnp.where(cand >= tau_ref[hd], e1_ref[hd, k1:k1 + 1, :] * e2_ref[hd], 0.0)
        p_ref[rows, :] = (g * _gelu_tanh(act[rows, :])).astype(BF16)


def _peer_dense_kernel(h_ref, u_ref, vtp_ref, vtc_ref, s1_ref, e1_ref, s2_ref, e2_ref, tau_ref, o_ref,
                       acc_sc, pa_sc, pb_sc, *, nsteps):
    j = pl.program_id(1)
    gate_refs = (s1_ref, e1_ref, s2_ref, e2_ref, tau_ref)

    @pl.when(j == 0)
    def _():
        acc_sc[...] = jnp.zeros(acc_sc.shape, F32)
        pb_sc[...] = jnp.zeros(pb_sc.shape, BF16)

    @pl.when(j < nsteps)
    def _():
        h = h_ref[...]
        act_a = _qk(u_ref[:PEER_TE, :], h)
        acc_sc[...] += jnp.dot(vtp_ref[...], pb_sc[...], preferred_element_type=F32)
        act_b = _qk(u_ref[PEER_TE:, :], h)
        _peer_gates(act_a, 0, *gate_refs, pa_sc)
        acc_sc[...] += jnp.dot(vtc_ref[...], pa_sc[...], preferred_element_type=F32)
        _peer_gates(act_b, PEER_TE // PEER_N_KEYS, *gate_refs, pb_sc)

    @pl.when(j == nsteps)
    def _():
        o_ref[...] = (acc_sc[...] + jnp.dot(vtp_ref[...], pb_sc[...], preferred_element_type=F32)).T


def peer_dense(h, u, vt, s1, s2, e1, e2, tau):
    M, D = h.shape
    E = u.shape[0]
    tm = _pick(M, (512, 384, 256, 128))
    te = PEER_TE
    ns = E // (2 * te)
    H, K = PEER_HEADS, PEER_N_KEYS
    kstep = PEER_K1_PER_STEP
    s1b = s1.reshape(H, K // kstep, kstep, M)
    e1b = e1.reshape(H, K // kstep, kstep, M)
    big = pl.BlockSpec((H, K, tm), lambda i, j: (0, 0, i))
    small = pl.BlockSpec((H, None, kstep, tm), lambda i, j: (0, jnp.minimum(j, ns - 1), 0, i))
    return pl.pallas_call(
        functools.partial(_peer_dense_kernel, nsteps=ns),
        grid=(M // tm, ns + 1),
        in_specs=[pl.BlockSpec((tm, D), lambda i, j: (i, 0)),
                  pl.BlockSpec((2 * te, D), lambda i, j: (jnp.minimum(j, ns - 1), 0)),
                  pl.BlockSpec((D, te), lambda i, j: (0, jnp.maximum(2 * j - 1, 0))),
                  pl.BlockSpec((D, te), lambda i, j: (0, jnp.minimum(2 * j, 2 * ns - 1))),
                  small, small, big, big,
                  pl.BlockSpec((H, 1, tm), lambda i, j: (0, 0, i))],
        out_specs=pl.BlockSpec((tm, D), lambda i, j: (i, 0)),
        out_shape=jax.ShapeDtypeStruct((M, D), F32),
        scratch_shapes=[pltpu.VMEM((D, tm), F32), pltpu.VMEM((te, tm), BF16), pltpu.VMEM((te, tm), BF16)],
        compiler_params=_cparams(("parallel", "arbitrary")),
        name="peer_dense",
    )(h, u, vt, vt, s1b, e1b, s2, e2, tau)


def peer_ffn(h, w_q, k1, k2, u, vt):
    q = matmul(h, w_q, BF16, "peer_q")
    s1, s2, e1, e2, tau = peer_route(q, k1, k2)
    return peer_dense(h, u, vt, s1, s2, e1, e2, tau)


def _mla_weights(w_a, w_uq, w_ukv):
    H, dn, dr, dv = MLA_HEADS, MLA_NOPE, MLA_ROPE, MLA_V
    pad_a = jnp.zeros((w_a.shape[0], LANES - dr), w_a.dtype)
    w_a_p = jnp.concatenate([w_a, pad_a], axis=1).astype(BF16)
    wq = w_uq.reshape(MLA_Q_RANK, H, dn + dr)
    wq_n = wq[:, :, :dn].reshape(MLA_Q_RANK, H * dn)
    wq_r = jnp.concatenate([wq[:, :, dn:], jnp.zeros((MLA_Q_RANK, H, LANES - dr), wq.dtype)], axis=2)
    w_q_p = jnp.concatenate([wq_n, wq_r.reshape(MLA_Q_RANK, H * LANES)], axis=1).astype(BF16)
    wkv = w_ukv.reshape(MLA_KV_RANK, H, dn + dv)
    w_kv_p = jnp.concatenate([wkv[:, :, :dn].reshape(MLA_KV_RANK, H * dn),
                              wkv[:, :, dn:].reshape(MLA_KV_RANK, H * dv)], axis=1).astype(BF16)
    return w_a_p, w_q_p, w_kv_p


def mla_mixer(h, w_a, q_g, kv_g, w_uq, w_ukv, w_o, tables, n_lat, n_ctx):
    B, S, D = h.shape
    M = B * S
    w_a_p, w_q_p, w_kv_p = _mla_weights(w_a, w_uq, w_ukv)
    ca = matmul(h.reshape(M, D), w_a_p, F32, "mla_a").reshape(B, S, -1)
    qc, kvc, krp = mla_prep(ca, q_g, kv_g, tables)
    q = matmul(qc.reshape(M, -1), w_q_p, BF16, "mla_q").reshape(B, S, -1)
    kv = matmul(kvc.reshape(M, -1), w_kv_p, BF16, "mla_kv").reshape(B, S, -1)
    o = mla_attention_all(q, kv, krp, tables, n_lat, n_ctx)
    return matmul(o.reshape(M, -1), w_o.astype(BF16), F32, "mla_o").reshape(B, S, D)


def ssd_mixer(h, w_in, conv_w, conv_b, a_log, dt_bias, d_skip, norm_g, w_out, n_lat):
    B, S, D = h.shape
    M = B * S
    G, HG = SSM_GROUPS, SSM_HG
    h2 = h.reshape(M, D)
    w_bf = w_in.astype(BF16)
    z = matmul(h2, w_bf[:, :SSM_D_INNER], BF16, "ssd_in_z").reshape(B, S, -1)
    xbc = matmul(h2, w_bf[:, SSM_D_INNER:SSM_D_INNER + SSM_CONV_DIM], BF16, "ssd_in_x").reshape(B, S, -1)
    dt_raw = matmul(h2, w_bf[:, SSM_D_INNER + SSM_CONV_DIM:], F32, "ssd_in_dt").reshape(B, S, -1)
    xbc_act, dt = ssd_conv(xbc, dt_raw, conv_w, conv_b, dt_bias.reshape(-1), n_lat)
    dt5 = dt.reshape(B, S, 2, G, HG).transpose(0, 2, 3, 1, 4)
    dtt5 = dt5.transpose(0, 1, 2, 4, 3)
    a = -jnp.exp(a_log.astype(F32)).reshape(2, G, HG)
    xt = jnp.swapaxes(xbc_act[:, :, :SSM_D_INNER], 1, 2)
    y2 = ssd_scan(xbc_act, xt, dt5, dtt5, a[:, :, None, :], a[:, :, :, None], n_lat)
    d_full = jnp.repeat(d_skip.astype(F32), SSM_HEADDIM)
    yn = ssd_finish(y2, xbc_act, z, d_full, norm_g)
    return matmul(yn.reshape(M, -1), w_out.astype(BF16), F32, "ssd_out").reshape(B, S, D)


def window_mixer(h, w_qkv, sinks, w_o, tables, n_lat, n_ctx):
    B, S, D = h.shape
    M = B * S
    qkv = matmul(h.reshape(M, D), w_qkv.astype(BF16), BF16, "win_qkv").reshape(B, S, -1)
    qk = rope_cols(qkv, (WQ_HEADS + WKV_HEADS) * W_HEAD_DIM, tables)
    o = window_attention(qk, qkv, sinks.astype(F32), n_lat, n_ctx)
    return matmul(o.reshape(M, -1), w_o.astype(BF16), F32, "win_o").reshape(B, S, D)


def kernel(x, c, ctx, c_ctx, ada_w, ada_b, norm_mix_g, norm_ffn_g, mla_w_a, mla_q_norm_g, mla_kv_norm_g,
           mla_w_uq, mla_w_ukv, mla_w_o, ssd_w_in, ssd_conv_w, ssd_conv_b, ssd_a_log, ssd_dt_bias,
           ssd_d_skip, ssd_norm_g, ssd_w_out, win_w_qkv, win_sinks, win_w_o, peer_w_q, peer_k1, peer_k2,
           peer_u, peer_v, final_norm_g):
    B, L, D = x.shape
    Lc = ctx.shape[1]
    S = L + Lc
    M = B * S
    depth = ada_w.shape[0]
    xs = jnp.concatenate([x, ctx], axis=1)

    n_grp = 8
    cmat = jnp.zeros((n_grp, D), F32).at[:B].set(c).at[B].set(c_ctx)
    mod_all = adaln_all(cmat, ada_w, ada_b).reshape(depth, n_grp, ADA_CHUNKS, 1, D)

    tab_mla = rope_tables(L, Lc, repeat_upper=False)
    tab_win = rope_tables(L, Lc, repeat_upper=True)

    h = None
    ffn = None
    for i in range(depth):
        mod = mod_all[i]
        if i == 0:
            (h,) = resid_norm_mod(xs, None, None, None, norm_mix_g[i], mod, 0, 1, L)
        else:
            xs, h = resid_norm_mod(xs, ffn, mod_all[i - 1], 5, norm_mix_g[i], mod, 0, 1, L)
        kind, j = i % N_MIXERS, i // N_MIXERS
        if kind == 0:
            mix = mla_mixer(h, mla_w_a[j], mla_q_norm_g[j], mla_kv_norm_g[j], mla_w_uq[j], mla_w_ukv[j],
                            mla_w_o[j], tab_mla, L, Lc)
        elif kind == 1:
            mix = ssd_mixer(h, ssd_w_in[j], ssd_conv_w[j], ssd_conv_b[j], ssd_a_log[j], ssd_dt_bias[j],
                            ssd_d_skip[j], ssd_norm_g[j], ssd_w_out[j], L)
        else:
            mix = window_mixer(h, win_w_qkv[j], win_sinks[j], win_w_o[j], tab_win, L, Lc)
        xs, h2 = resid_norm_mod(xs, mix, mod, 2, norm_ffn_g[i], mod, 3, 4, L)
        ffn = peer_ffn(h2.reshape(M, D), peer_w_q[i].astype(BF16),
                       peer_k1[i].astype(BF16), peer_k2[i].astype(BF16),
                       peer_u[i].astype(BF16), peer_v[i].T.astype(BF16)).reshape(B, S, D)
    (out,) = resid_norm_mod(xs, ffn, mod_all[depth - 1], 5, final_norm_g, None, None, None, L, final=True)
    return out
```

```python
import functools
import math

import jax
import jax.numpy as jnp
from jax import lax
from jax.experimental import pallas as pl
from jax.experimental.pallas import tpu as pltpu

F32 = jnp.float32
BF16 = jnp.bfloat16

D_MODEL = 2048
DEPTH = 4
GRID_W = 64
N_MIXERS = 3
NORM_EPS = 1e-6
ROPE_THETA = 10000.0
NEG_INF = -1e30
ADA_CHUNKS = 6

MLA_HEADS = 16
MLA_Q_RANK = 512
MLA_KV_RANK = 512
MLA_NOPE = 128
MLA_ROPE = 64
MLA_V = 128
MLA_VT_ROWS = MLA_V + 16

SSM_D_INNER = 2 * D_MODEL
SSM_HEADDIM = 64
SSM_HEADS = SSM_D_INNER // SSM_HEADDIM
SSM_GROUPS = 8
SSM_HG = SSM_HEADS // SSM_GROUPS
SSM_STATE = 128
SSM_CONV = 5
SSM_CHUNK = 128
SSM_BC = SSM_GROUPS * SSM_STATE
SSM_CONV_DIM = SSM_D_INNER + 2 * SSM_BC

WINDOW = 128
WQ_HEADS = 32
WKV_HEADS = 8
W_HEAD_DIM = 64

PEER_HEADS = 8
PEER_N_KEYS = 128
PEER_N_EXPERTS = PEER_N_KEYS * PEER_N_KEYS
PEER_TOPK = 16
PEER_HALF = 128

LANES = 128
VMEM_LIMIT = 56 * 1024 * 1024
ROW_TILE = 256


def _pick(n, candidates):
    for c in candidates:
        if n % c == 0:
            return c
    raise ValueError(f"no tile for {n} in {candidates}")


def _cparams(sem):
    return pltpu.CompilerParams(dimension_semantics=sem, vmem_limit_bytes=VMEM_LIMIT)


def _mm_kernel(a_ref, w_ref, o_ref):
    o_ref[...] = jnp.dot(a_ref[...], w_ref[...], preferred_element_type=F32).astype(o_ref.dtype)


def matmul(a, w, out_dtype, name):
    M, K = a.shape
    N = w.shape[1]
    tm = _pick(M, (1056, 1024, 768, 512, 384, 256, 128, 64, 32, 16))
    tn = _pick(N, (512, 384, 256, 128))
    return pl.pallas_call(
        _mm_kernel,
        grid=(M // tm, N // tn),
        in_specs=[pl.BlockSpec((tm, K), lambda i, j: (i, 0)),
                  pl.BlockSpec((K, tn), lambda i, j: (0, j))],
        out_specs=pl.BlockSpec((tm, tn), lambda i, j: (i, j)),
        out_shape=jax.ShapeDtypeStruct((M, N), out_dtype),
        compiler_params=_cparams(("parallel", "parallel")),
        name=name,
    )(a, w)


def _adaln_kernel(c_ref, w_ref, b_ref, o_ref):
    c = c_ref[...]
    s = c * (1.0 / (1.0 + jnp.exp(-c)))
    o_ref[...] = jnp.dot(s, w_ref[...], preferred_element_type=F32,
                         precision=lax.Precision.HIGHEST) + b_ref[...]


def adaln_all(cmat, ada_w, ada_b):
    depth, d, n = ada_w.shape
    rows = cmat.shape[0]
    tn = _pick(n, (1024, 512, 256, 128))
    return pl.pallas_call(
        _adaln_kernel,
        grid=(depth, n // tn),
        in_specs=[pl.BlockSpec((rows, d), lambda l, j: (0, 0)),
                  pl.BlockSpec((None, d, tn), lambda l, j: (l, 0, j)),
                  pl.BlockSpec((None, 1, tn), lambda l, j: (l, 0, j))],
        out_specs=pl.BlockSpec((None, rows, tn), lambda l, j: (l, 0, j)),
        out_shape=jax.ShapeDtypeStruct((depth, rows, n), F32),
        compiler_params=_cparams(("parallel", "parallel")),
        name="adaln",
    )(cmat, ada_w, ada_b.reshape(depth, 1, n))


def _rms(x, g):
    var = jnp.mean(x * x, axis=-1, keepdims=True)
    return x * lax.rsqrt(var + NORM_EPS) * g


def _rnm_kernel(*refs, has_mix, modulate):
    it = iter(refs)
    x_ref = next(it)
    if has_mix:
        mix_ref, gate_ref = next(it), next(it)
    g_ref = next(it)
    if modulate:
        shift_ref, scale_ref = next(it), next(it)
    if has_mix and modulate:
        xo_ref = next(it)
    h_ref = next(it)
    x = x_ref[...]
    if has_mix:
        x = x + gate_ref[...] * mix_ref[...].astype(F32)
        if modulate:
            xo_ref[...] = x
    y = _rms(x, g_ref[...])
    if modulate:
        y = y * (1.0 + scale_ref[...]) + shift_ref[...]
    h_ref[...] = y.astype(h_ref.dtype)


def resid_norm_mod(x, mix, gate_mod, gate_k, norm_g, mod, shift_k, scale_k, n_lat, *, final=False):
    B, S, D = x.shape
    tr = ROW_TILE
    lat_tiles = n_lat // tr
    tiles = lat_tiles if final else S // tr
    has_mix = mix is not None
    modulate = not final

    def grp(b, t):
        return jnp.where(t < lat_tiles, b, B)

    row = pl.BlockSpec((None, tr, D), lambda b, t: (b, t, 0))

    def modspec(k):
        return pl.BlockSpec((None, None, 1, D), lambda b, t: (grp(b, t), k, 0, 0))

    args, specs = [x], [row]
    if has_mix:
        args += [mix, gate_mod]
        specs += [row, modspec(gate_k)]
    args.append(norm_g.reshape(1, D))
    specs.append(pl.BlockSpec((1, D), lambda b, t: (0, 0)))
    if modulate:
        args += [mod, mod]
        specs += [modspec(shift_k), modspec(scale_k)]
    out_shapes, out_specs = [], []
    if has_mix and modulate:
        out_shapes.append(jax.ShapeDtypeStruct((B, S, D), F32))
        out_specs.append(row)
    if final:
        out_shapes.append(jax.ShapeDtypeStruct((B, n_lat, D), F32))
    else:
        out_shapes.append(jax.ShapeDtypeStruct((B, S, D), BF16))
    out_specs.append(row)
    outs = pl.pallas_call(
        functools.partial(_rnm_kernel, has_mix=has_mix, modulate=modulate),
        grid=(B, tiles),
        in_specs=specs,
        out_specs=out_specs,
        out_shape=out_shapes,
        compiler_params=_cparams(("parallel", "parallel")),
        name="resid_norm_mod",
    )(*args)
    return outs


def rope_tables(n_lat, n_ctx, repeat_upper):
    half = MLA_ROPE // 4
    pos = jnp.arange(n_lat, dtype=jnp.int32)
    rows = (pos // GRID_W).astype(F32)
    cols = (pos % GRID_W).astype(F32)
    freqs = ROPE_THETA ** (-jnp.arange(half, dtype=F32) / half)
    ang_r = rows[:, None] * freqs
    ang_c = cols[:, None] * freqs
    ang = jnp.concatenate([ang_r, ang_r, ang_c, ang_c], axis=-1)
    cos, sin = jnp.cos(ang), jnp.sin(ang)
    first = (jnp.arange(64) % 32) < 16
    sa = jnp.where(first, -sin, 0.0)
    sb = jnp.where(first, 0.0, sin)
    cos = jnp.concatenate([cos, jnp.ones((n_ctx, 64), F32)], axis=0)
    sa = jnp.concatenate([sa, jnp.zeros((n_ctx, 64), F32)], axis=0)
    sb = jnp.concatenate([sb, jnp.zeros((n_ctx, 64), F32)], axis=0)
    if repeat_upper:
        return tuple(jnp.concatenate([t, t], axis=-1) for t in (cos, sa, sb))
    z = jnp.zeros_like(cos)
    return tuple(jnp.concatenate([t, z], axis=-1) for t in (cos, sa, sb))


def _rope128(x, cos, sa, sb):
    return x * cos + pltpu.roll(x, 112, 1) * sa + pltpu.roll(x, 16, 1) * sb


def _rope_cols_kernel(x_ref, cos_ref, sa_ref, sb_ref, o_ref):
    cos, sa, sb = cos_ref[...], sa_ref[...], sb_ref[...]
    for c in range(x_ref.shape[-1] // LANES):
        sl = slice(c * LANES, (c + 1) * LANES)
        o_ref[:, sl] = _rope128(x_ref[:, sl].astype(F32), cos, sa, sb).astype(o_ref.dtype)


def rope_cols(x, width, tables):
    B, S, _ = x.shape
    tr = ROW_TILE
    tab = pl.BlockSpec((tr, LANES), lambda b, t: (t, 0))
    return pl.pallas_call(
        _rope_cols_kernel,
        grid=(B, S // tr),
        in_specs=[pl.BlockSpec((None, tr, width), lambda b, t: (b, t, 0)), tab, tab, tab],
        out_specs=pl.BlockSpec((None, tr, width), lambda b, t: (b, t, 0)),
        out_shape=jax.ShapeDtypeStruct((B, S, width), BF16),
        compiler_params=_cparams(("parallel", "parallel")),
        name="rope_cols",
    )(x, *tables)


def _mla_prep_kernel(ca_ref, qg_ref, kvg_ref, cos_ref, sa_ref, sb_ref, qc_ref, kvc_ref, kr_ref):
    cq = ca_ref[:, :MLA_Q_RANK]
    ckv = ca_ref[:, MLA_Q_RANK:MLA_Q_RANK + MLA_KV_RANK]
    kr = ca_ref[:, MLA_Q_RANK + MLA_KV_RANK:]
    qc_ref[...] = _rms(cq, qg_ref[...]).astype(BF16)
    kvc_ref[...] = _rms(ckv, kvg_ref[...]).astype(BF16)
    kr_ref[...] = _rope128(kr, cos_ref[...], sa_ref[...], sb_ref[...]).astype(BF16)


def mla_prep(ca, q_g, kv_g, tables):
    B, S, C = ca.shape
    tr = ROW_TILE
    tab = pl.BlockSpec((tr, LANES), lambda b, t: (t, 0))

    def row(w):
        return pl.BlockSpec((None, tr, w), lambda b, t: (b, t, 0))

    def vec(w):
        return pl.BlockSpec((1, w), lambda b, t: (0, 0))

    return pl.pallas_call(
        _mla_prep_kernel,
        grid=(B, S // tr),
        in_specs=[row(C), vec(MLA_Q_RANK), vec(MLA_KV_RANK), tab, tab, tab],
        out_specs=[row(MLA_Q_RANK), row(MLA_KV_RANK), row(LANES)],
        out_shape=[jax.ShapeDtypeStruct((B, S, MLA_Q_RANK), BF16),
                   jax.ShapeDtypeStruct((B, S, MLA_KV_RANK), BF16),
                   jax.ShapeDtypeStruct((B, S, LANES), BF16)],
        compiler_params=_cparams(("parallel", "parallel")),
        name="mla_prep",
    )(ca, q_g.reshape(1, -1), kv_g.reshape(1, -1), *tables)


def _mla_attn_kernel(qn_ref, qr_ref, cos_ref, sa_ref, sb_ref, kn_ref, kr_ref, vt_ref, o_ref,
                     q_sc, m_sc, acc_sc, *, qscale, nk, nsub):
    j = pl.program_id(3)
    tqs = q_sc.shape[0] // nsub

    @pl.when(j == 0)
    def _():
        qr = _rope128(qr_ref[...].astype(F32), cos_ref[...], sa_ref[...], sb_ref[...])
        q_sc[:, :LANES] = (qn_ref[...].astype(F32) * qscale).astype(BF16)
        q_sc[:, LANES:] = (qr * qscale).astype(BF16)
        m_sc[...] = jnp.full(m_sc.shape, -jnp.inf, F32)
        acc_sc[...] = jnp.zeros(acc_sc.shape, F32)

    k = jnp.concatenate([kn_ref[...], kr_ref[...]], axis=1)
    scores = [_qk(k, q_sc[u * tqs:(u + 1) * tqs, :]) for u in range(nsub)]
    for u in range(nsub):
        cols = slice(u * tqs, (u + 1) * tqs)
        s = scores[u]
        m_prev = m_sc[:, cols]
        m_new = jnp.maximum(m_prev, jnp.max(s, axis=0, keepdims=True))
        alpha = jnp.exp2(m_prev - m_new)
        p = jnp.exp2((s - m_new).astype(BF16))
        acc_sc[:, cols] = alpha * acc_sc[:, cols] + jnp.dot(vt_ref[...], p, preferred_element_type=F32)
        m_sc[:, cols] = m_new

    @pl.when(j == nk - 1)
    def _():
        o_ref[...] = (acc_sc[:MLA_V, :] / acc_sc[MLA_V:MLA_V + 1, :]).T.astype(o_ref.dtype)


def _qk(q, k):
    return lax.dot_general(q, k, (((1,), (1,)), ((), ())), preferred_element_type=F32)


def mla_attention(q, kv, vt, krp, tables, o_prev, *, q_start, q_len, k_start, k_len):
    B, S, _ = q.shape
    H = MLA_HEADS
    tq = _pick(q_len, (2048, 1024, 512, 256))
    nsub = max(tq // 512, 1)
    tk = _pick(k_len, (1408, 768, 512, 256))
    assert q_start % tq == 0 and k_start % tk == 0
    qo, ko = q_start // tq, k_start // tk
    nq, nk = q_len // tq, k_len // tk
    qscale = (MLA_NOPE + MLA_ROPE) ** -0.5 * math.log2(math.e)
    tab = pl.BlockSpec((tq, LANES), lambda b, h, i, j: (i + qo, 0))
    in_specs = [
        pl.BlockSpec((None, tq, LANES), lambda b, h, i, j: (b, i + qo, h)),
        pl.BlockSpec((None, tq, LANES), lambda b, h, i, j: (b, i + qo, H + h)),
        tab, tab, tab,
        pl.BlockSpec((None, tk, LANES), lambda b, h, i, j: (b, j + ko, h)),
        pl.BlockSpec((None, tk, LANES), lambda b, h, i, j: (b, j + ko, 0)),
        pl.BlockSpec((None, None, MLA_VT_ROWS, tk), lambda b, h, i, j: (b, h, 0, j + ko)),
    ]
    args = [q, q, *tables, kv, krp, vt]
    aliases = {}
    if o_prev is not None:
        in_specs.append(pl.BlockSpec(memory_space=pl.ANY))
        args.append(o_prev)
        aliases = {len(args) - 1: 0}

    def body(*refs):
        if o_prev is not None:
            refs = refs[:8] + refs[9:]
        _mla_attn_kernel(*refs, qscale=qscale, nk=nk, nsub=nsub)

    return pl.pallas_call(
        body,
        grid=(B, H, nq, nk),
        in_specs=in_specs,
        out_specs=pl.BlockSpec((None, tq, LANES), lambda b, h, i, j: (b, i + qo, h)),
        out_shape=jax.ShapeDtypeStruct((B, S, H * MLA_V), BF16),
        scratch_shapes=[pltpu.VMEM((tq, 2 * LANES), BF16), pltpu.VMEM((1, tq), F32),
                        pltpu.VMEM((MLA_VT_ROWS, tq), F32)],
        input_output_aliases=aliases,
        compiler_params=_cparams(("parallel", "parallel", "parallel", "arbitrary")),
        name="mla_attn",
    )(*args)


def mla_attention_all(q, kv, krp, tables, n_lat, n_ctx):
    B, S, _ = q.shape
    H = MLA_HEADS
    vt = jnp.swapaxes(kv[:, :, H * MLA_NOPE:], 1, 2).reshape(B, H, MLA_V, S)
    vt = jnp.concatenate([vt, jnp.ones((B, H, MLA_VT_ROWS - MLA_V, S), BF16)], axis=2)
    o = mla_attention(q, kv, vt, krp, tables, None, q_start=0, q_len=n_lat, k_start=0, k_len=S)
    return mla_attention(q, kv, vt, krp, tables, o, q_start=n_lat, q_len=n_ctx, k_start=n_lat, k_len=n_ctx)


def _softmax_pv(s_list, v_list, sink):
    m = sink
    for s in s_list:
        m = jnp.maximum(m, jnp.max(s, axis=1, keepdims=True))
    den = jnp.exp(sink - m)
    o = None
    for s, v in zip(s_list, v_list):
        p = jnp.exp(s - m)
        den = den + jnp.sum(p, axis=1, keepdims=True)
        pv = jnp.dot(p.astype(BF16), v, preferred_element_type=F32)
        o = pv if o is None else o + pv
    return o / den


def _win_attn_kernel(sink_ref, q_ref, kp_ref, kc_ref, kn_ref, kx_ref, vp_ref, vc_ref, vn_ref, vx_ref,
                     o_ref, *, scale, nblk):
    p_idx = pl.program_id(1)
    i = pl.program_id(2)
    r = lax.broadcasted_iota(jnp.int32, (WINDOW, WINDOW), 0)
    c = lax.broadcasted_iota(jnp.int32, (WINDOW, WINDOW), 1)
    prev_ok = jnp.logical_and(c >= r, i > 0)
    next_ok = jnp.logical_and(c <= r, i < nblk - 1)
    hd = W_HEAD_DIM
    scores = []
    for hh in range(8):
        ks = slice((hh // 4) * hd, (hh // 4 + 1) * hd)
        q = q_ref[:, hh * hd:(hh + 1) * hd]
        scores.append((jnp.where(prev_ok, _qk(q, kp_ref[:, ks]) * scale, NEG_INF),
                       _qk(q, kc_ref[:, ks]) * scale,
                       jnp.where(next_ok, _qk(q, kn_ref[:, ks]) * scale, NEG_INF),
                       _qk(q, kx_ref[:, ks]) * scale))
    for hh in range(8):
        ks = slice((hh // 4) * hd, (hh // 4 + 1) * hd)
        sink = jnp.full((1, 1), sink_ref[p_idx * 8 + hh], F32)
        o = _softmax_pv(list(scores[hh]), [vp_ref[:, ks], vc_ref[:, ks], vn_ref[:, ks], vx_ref[:, ks]], sink)
        o_ref[:, hh * hd:(hh + 1) * hd] = o.astype(o_ref.dtype)


def _ctx_attn_kernel(sink_ref, q_ref, kx_ref, vx_ref, o_ref, *, scale):
    p_idx = pl.program_id(1)
    hd = W_HEAD_DIM
    for hh in range(8):
        kvh = hh // 4
        ks = slice(kvh * hd, (kvh + 1) * hd)
        q = q_ref[:, hh * hd:(hh + 1) * hd]
        sink = jnp.full((1, 1), sink_ref[p_idx * 8 + hh], F32)
        s_x = _qk(q, kx_ref[:, ks]) * scale
        o = _softmax_pv([s_x], [vx_ref[:, ks]], sink)
        o_ref[:, hh * hd:(hh + 1) * hd] = o.astype(o_ref.dtype)


def window_attention(qk, qkv, sinks, n_lat, n_ctx):
    B, S, _ = qkv.shape
    W = WINDOW
    nblk = n_lat // W
    scale = W_HEAD_DIM ** -0.5
    kcol = WQ_HEADS * W_HEAD_DIM // LANES
    vcol = (WQ_HEADS + WKV_HEADS) * W_HEAD_DIM // LANES
    assert n_lat % n_ctx == 0
    xblk = n_lat // n_ctx
    qw = 8 * W_HEAD_DIM
    smem = pl.BlockSpec(memory_space=pltpu.SMEM)

    def kspec(col, shift):
        return pl.BlockSpec((None, W, LANES),
                            lambda b, p, i: (b, jnp.clip(i + shift, 0, nblk - 1), col + p))

    def xspec(col):
        return pl.BlockSpec((None, n_ctx, LANES), lambda b, p, i: (b, xblk, col + p))

    o_lat = pl.pallas_call(
        functools.partial(_win_attn_kernel, scale=scale, nblk=nblk),
        grid=(B, WKV_HEADS // 2, nblk),
        in_specs=[smem, pl.BlockSpec((None, W, qw), lambda b, p, i: (b, i, p)),
                  kspec(kcol, -1), kspec(kcol, 0), kspec(kcol, 1), xspec(kcol),
                  kspec(vcol, -1), kspec(vcol, 0), kspec(vcol, 1), xspec(vcol)],
        out_specs=pl.BlockSpec((None, W, qw), lambda b, p, i: (b, i, p)),
        out_shape=jax.ShapeDtypeStruct((B, S, WQ_HEADS * W_HEAD_DIM), BF16),
        compiler_params=_cparams(("parallel", "parallel", "parallel")),
        name="win_attn",
    )(sinks, qk, qk, qk, qk, qk, qkv, qkv, qkv, qkv)

    def ctx_body(sink_ref, q_ref, kx_ref, vx_ref, _, o_ref):
        _ctx_attn_kernel(sink_ref, q_ref, kx_ref, vx_ref, o_ref, scale=scale)

    return pl.pallas_call(
        ctx_body,
        grid=(B, WKV_HEADS // 2, 1),
        in_specs=[smem, pl.BlockSpec((None, n_ctx, qw), lambda b, p, i: (b, xblk, p)),
                  xspec(kcol), xspec(vcol), pl.BlockSpec(memory_space=pl.ANY)],
        out_specs=pl.BlockSpec((None, n_ctx, qw), lambda b, p, i: (b, xblk, p)),
        out_shape=jax.ShapeDtypeStruct((B, S, WQ_HEADS * W_HEAD_DIM), BF16),
        input_output_aliases={4: 0},
        compiler_params=_cparams(("parallel", "parallel", "parallel")),
        name="win_attn_ctx",
    )(sinks, qk, qk, qkv, o_lat)


HALO = 16


def _conv_kernel(x_ref, xp_ref, xn_ref, w_ref, b_ref, dtr_ref, dtb_ref, o_ref, dt_ref, ext_sc,
                 *, lat_tiles, tiles):
    t = pl.program_id(1)
    tr = x_ref.shape[0]
    at_start = jnp.logical_or(t == 0, t == lat_tiles)
    at_end = jnp.logical_or(t == lat_tiles - 1, t == tiles - 1)
    ext_sc[0:HALO, :] = jnp.where(at_start, 0.0, xp_ref[...].astype(F32))
    ext_sc[HALO:HALO + tr, :] = x_ref[...].astype(F32)
    ext_sc[HALO + tr:, :] = jnp.where(at_end, 0.0, xn_ref[...].astype(F32))
    pad = (SSM_CONV - 1) // 2
    acc = jnp.broadcast_to(b_ref[...], (tr, x_ref.shape[1]))
    for k in range(SSM_CONV):
        acc = acc + w_ref[k:k + 1, :] * ext_sc[HALO - pad + k:HALO - pad + k + tr, :]
    o_ref[...] = (acc * (1.0 / (1.0 + jnp.exp(-acc)))).astype(o_ref.dtype)
    z = dtr_ref[...] + dtb_ref[...]
    dt_ref[...] = jnp.maximum(z, 0.0) + jnp.log(1.0 + jnp.exp(-jnp.abs(z)))


def ssd_conv(xbc, dt_raw, conv_w, conv_b, dt_bias, n_lat):
    B, S, C = xbc.shape
    tr = ROW_TILE
    lat_tiles, tiles = n_lat // tr, S // tr
    hb = tr // HALO
    nh = S // HALO
    row = pl.BlockSpec((None, tr, C), lambda b, t: (b, t, 0))
    dtrow = pl.BlockSpec((None, tr, LANES), lambda b, t: (b, t, 0))
    return pl.pallas_call(
        functools.partial(_conv_kernel, lat_tiles=lat_tiles, tiles=tiles),
        grid=(B, tiles),
        in_specs=[row,
                  pl.BlockSpec((None, HALO, C), lambda b, t: (b, jnp.maximum(t * hb - 1, 0), 0)),
                  pl.BlockSpec((None, HALO, C), lambda b, t: (b, jnp.minimum((t + 1) * hb, nh - 1), 0)),
                  pl.BlockSpec((SSM_CONV, C), lambda b, t: (0, 0)),
                  pl.BlockSpec((1, C), lambda b, t: (0, 0)),
                  dtrow,
                  pl.BlockSpec((1, LANES), lambda b, t: (0, 0))],
        out_specs=[row, dtrow],
        out_shape=[jax.ShapeDtypeStruct((B, S, C), BF16), jax.ShapeDtypeStruct((B, S, LANES), F32)],
        scratch_shapes=[pltpu.VMEM((tr + 2 * HALO, C), F32)],
        compiler_params=_cparams(("parallel", "parallel")),
        name="ssd_conv",
    )(xbc, xbc, xbc, conv_w, conv_b.reshape(1, C), dt_raw, dt_bias.reshape(1, LANES))


SSD_GROUPS_PER_STEP = 4


def _ssd_scan_kernel(x_ref, xt_ref, b_ref, c_ref, dt_ref, dtt_ref, a_ref, at_ref, y_ref, st_sc):
    d = pl.program_id(1)
    s = pl.program_id(3)
    Q = SSM_CHUNK
    P = SSM_HEADDIM

    @pl.when(s == 0)
    def _():
        st_sc[...] = jnp.zeros(st_sc.shape, F32)

    sgn = 1 - 2 * d
    ri = lax.broadcasted_iota(jnp.int32, (Q, Q), 0)
    ci = lax.broadcasted_iota(jnp.int32, (Q, Q), 1)
    mask = (ri - ci) * sgn >= 0
    tri = mask.astype(F32)
    tri_t = ((ci - ri) * sgn >= 0).astype(F32)

    hi = lax.Precision.HIGHEST
    N = SSM_STATE
    for gg in range(SSD_GROUPS_PER_STEP):
        dt = dt_ref[gg]
        dtt = dtt_ref[gg]
        a = dt * a_ref[gg]
        a_t = dtt * at_ref[gg]
        acs = jnp.dot(tri, a, preferred_element_type=F32, precision=hi)
        acs_t = jnp.dot(a_t, tri_t, preferred_element_type=F32, precision=hi)
        tot_c = jnp.sum(a_t, axis=1, keepdims=True)
        w_t = jnp.exp(tot_c - acs_t) * dtt
        e_tot = jnp.exp(tot_c)

        bq = b_ref[:, gg * N:(gg + 1) * N]
        cq = c_ref[:, gg * N:(gg + 1) * N]
        cb = _qk(cq, bq)
        for hh in range(SSM_HG):
            hx = gg * SSM_HG + hh
            cols = slice(hx * P, (hx + 1) * P)
            acs_c = acs[:, hh:hh + 1]
            seg = acs_c - acs_t[hh:hh + 1, :]
            decay = jnp.exp(jnp.where(mask, seg, -jnp.inf))
            m = (cb * decay * dtt[hh:hh + 1, :]).astype(BF16)
            y = jnp.dot(m, x_ref[:, cols], preferred_element_type=F32)
            st = st_sc[hx]
            y_off = _qk(cq, st.astype(BF16))
            y_ref[:, cols] = (y + y_off * jnp.exp(acs_c)).astype(y_ref.dtype)
            xw = (xt_ref[cols, :].astype(F32) * w_t[hh:hh + 1, :]).astype(BF16)
            upd = jnp.dot(xw, bq, preferred_element_type=F32)
            st_sc[hx] = e_tot[hh:hh + 1, :] * st + upd


def ssd_scan(xbc_act, xt, dt5, dtt5, a_row, a_col, n_lat):
    B, S, _ = xbc_act.shape
    Q = SSM_CHUNK
    G, HG, P, N = SSM_GROUPS, SSM_HG, SSM_HEADDIM, SSM_STATE
    gs = SSD_GROUPS_PER_STEP
    nch = S // Q
    latc = n_lat // Q
    ncc = nch - latc
    gw = gs * HG * P
    bcol = SSM_D_INNER // (gs * N)
    ccol = bcol + G // gs

    def chunk(d, s):
        fwd = jnp.where(s < ncc, latc + s, s - ncc)
        return jnp.where(d == 0, fwd, nch - 1 - s)

    return pl.pallas_call(
        _ssd_scan_kernel,
        grid=(B, 2, G // gs, nch),
        in_specs=[
            pl.BlockSpec((None, Q, gw), lambda b, d, g, s: (b, chunk(d, s), g)),
            pl.BlockSpec((None, gw, Q), lambda b, d, g, s: (b, g, chunk(d, s))),
            pl.BlockSpec((None, Q, gs * N), lambda b, d, g, s: (b, chunk(d, s), bcol + g)),
            pl.BlockSpec((None, Q, gs * N), lambda b, d, g, s: (b, chunk(d, s), ccol + g)),
            pl.BlockSpec((None, None, gs, Q, HG), lambda b, d, g, s: (b, d, g, chunk(d, s), 0)),
            pl.BlockSpec((None, None, gs, HG, Q), lambda b, d, g, s: (b, d, g, 0, chunk(d, s))),
            pl.BlockSpec((None, gs, 1, HG), lambda b, d, g, s: (d, g, 0, 0)),
            pl.BlockSpec((None, gs, HG, 1), lambda b, d, g, s: (d, g, 0, 0)),
        ],
        out_specs=pl.BlockSpec((None, None, Q, gw), lambda b, d, g, s: (b, d, chunk(d, s), g)),
        out_shape=jax.ShapeDtypeStruct((B, 2, S, SSM_D_INNER), BF16),
        scratch_shapes=[pltpu.VMEM((gs * HG, P, N), F32)],
        compiler_params=_cparams(("parallel", "parallel", "parallel", "arbitrary")),
        name="ssd_scan",
    )(xbc_act, xt, xbc_act, xbc_act, dt5, dtt5, a_row, a_col)


def _ssd_finish_kernel(yf_ref, yb_ref, x_ref, z_ref, d_ref, g_ref, o_ref):
    y = yf_ref[...].astype(F32) + yb_ref[...].astype(F32) + d_ref[...] * x_ref[...].astype(F32)
    z = z_ref[...].astype(F32)
    y = y * (z * (1.0 / (1.0 + jnp.exp(-z))))
    o_ref[...] = _rms(y, g_ref[...]).astype(o_ref.dtype)


def ssd_finish(y2, xbc_act, z, d_full, norm_g):
    B, _, S, C = y2.shape
    tr = ROW_TILE
    row = pl.BlockSpec((None, tr, C), lambda b, t: (b, t, 0))
    vec = pl.BlockSpec((1, C), lambda b, t: (0, 0))
    return pl.pallas_call(
        _ssd_finish_kernel,
        grid=(B, S // tr),
        in_specs=[pl.BlockSpec((None, None, tr, C), lambda b, t: (b, 0, t, 0)),
                  pl.BlockSpec((None, None, tr, C), lambda b, t: (b, 1, t, 0)),
                  row, row, vec, vec],
        out_specs=row,
        out_shape=jax.ShapeDtypeStruct((B, S, C), BF16),
        compiler_params=_cparams(("parallel", "parallel")),
        name="ssd_finish",
    )(y2, y2, xbc_act, z, d_full.reshape(1, C), norm_g.reshape(1, C))


def _extract_topk(x, k):
    n = x.shape[0]
    iota = lax.broadcasted_iota(jnp.int32, x.shape, 0)
    work = x
    member = jnp.zeros(x.shape, F32)
    vals = []
    for _ in range(k):
        m = jnp.max(work, axis=0, keepdims=True)
        idx = jnp.min(jnp.where(work == m, iota, n), axis=0, keepdims=True)
        sel = iota == idx
        work = jnp.where(sel, -jnp.inf, work)
        member = jnp.where(sel, 1.0, member)
        vals.append(m)
    return jnp.concatenate(vals, axis=0), member, jnp.zeros((1, x.shape[1]), F32)


def _extract_topk_distinct(x, k):
    work = x
    vals = []
    for _ in range(k):
        m = jnp.max(work, axis=0, keepdims=True)
        work = jnp.where(work == m, -jnp.inf, work)
        vals.append(m)
    member = jnp.where(work != x, 1.0, 0.0)
    miss = jnp.abs(jnp.sum(member, axis=0, keepdims=True) - float(k))
    return jnp.concatenate(vals, axis=0), member, miss


_STAIR = [(a, b) for a in range(PEER_TOPK) for b in range(PEER_TOPK) if (a + 1) * (b + 1) <= PEER_TOPK]


def _route_head(s1, s2, extract):
    v1, m1, miss1 = extract(s1, PEER_TOPK)
    v2, m2, miss2 = extract(s2, PEER_TOPK)
    pad_rows = [jnp.full((-len(_STAIR) % 8, v1.shape[1]), -jnp.inf, F32)]
    cand = jnp.concatenate([v1[a:a + 1, :] + v2[b:b + 1, :] for a, b in _STAIR] + pad_rows, axis=0)
    top, sel, miss3 = extract(cand, PEER_TOPK)
    tau = top[PEER_TOPK - 1:PEER_TOPK, :]
    cmax = top[0:1, :]
    z = jnp.sum(sel * jnp.exp(cand - cmax), axis=0, keepdims=True)
    e1 = m1 * jnp.exp(s1 - v1[0:1, :]) / z
    e2 = m2 * jnp.exp(s2 - v2[0:1, :])
    return e1, e2, tau, miss1 + miss2 + miss3


def _peer_route_kernel(q_ref, k1_ref, k2_ref, s1_ref, s2_ref, e1_ref, e2_ref, tau_ref):
    misses = []
    for hd in range(PEER_HEADS):
        q1 = q_ref[:, (2 * hd) * PEER_HALF:(2 * hd + 1) * PEER_HALF]
        q2 = q_ref[:, (2 * hd + 1) * PEER_HALF:(2 * hd + 2) * PEER_HALF]
        s1 = _qk(k1_ref[hd], q1)
        s2 = _qk(k2_ref[hd], q2)
        s1_ref[hd] = s1
        s2_ref[hd] = s2
        e1, e2, tau, miss = _route_head(s1, s2, _extract_topk_distinct)
        e1_ref[hd] = e1
        e2_ref[hd] = e2
        tau_ref[hd] = tau
        misses.append(jnp.max(miss))

    for hd in range(PEER_HEADS):
        @pl.when(misses[hd] > 0.0)
        def _():
            e1x, e2x, taux, _ = _route_head(s1_ref[hd], s2_ref[hd], _extract_topk)
            e1_ref[hd] = e1x
            e2_ref[hd] = e2x
            tau_ref[hd] = taux


def peer_route(q, k1, k2):
    M = q.shape[0]
    tm = _pick(M, (256, 128))
    H, K = PEER_HEADS, PEER_N_KEYS
    big = pl.BlockSpec((H, K, tm), lambda i: (0, 0, i))
    kspec = pl.BlockSpec((H, K, PEER_HALF), lambda i: (0, 0, 0))
    sd = jax.ShapeDtypeStruct((H, K, M), F32)
    return pl.pallas_call(
        _peer_route_kernel,
        grid=(M // tm,),
        in_specs=[pl.BlockSpec((tm, 2 * H * PEER_HALF), lambda i: (i, 0)), kspec, kspec],
        out_specs=[big, big, big, big, pl.BlockSpec((H, 1, tm), lambda i: (0, 0, i))],
        out_shape=[sd, sd, sd, sd, jax.ShapeDtypeStruct((H, 1, M), F32)],
        compiler_params=_cparams(("parallel",)),
        name="peer_route",
    )(q, k1, k2)


def _gelu_tanh(x):
    c = 0.7978845608028654
    t = jnp.tanh(x * (c + (c * 0.044715) * (x * x)))
    hx = 0.5 * x
    return hx + hx * t


PEER_TE = 512
PEER_K1_PER_STEP = 2 * PEER_TE // PEER_N_KEYS


def _peer_gates(act, k1_base, s1_ref, e1_ref, s2_ref, e2_ref, tau_ref, p_ref):
    for kk in range(PEER_TE // PEER_N_KEYS):
        rows = slice(kk * PEER_N_KEYS, (kk + 1) * PEER_N_KEYS)
        k1 = k1_base + kk
        g = jnp.zeros((PEER_N_KEYS, act.shape[1]), F32)
        for hd in range(PEER_HEADS):
            cand = s1_ref[hd, k1:k1 + 1, :] + s2_ref[hd]
            g = g + jnp.where(cand >= tau_ref[hd], e1_ref[hd, k1:k1 + 1, :] * e2_ref[hd], 0.0)
        p_ref[rows, :] = (g * _gelu_tanh(act[rows, :])).astype(BF16)


def _peer_dense_kernel(h_ref, u_ref, vtp_ref, vtc_ref, s1_ref, e1_ref, s2_ref, e2_ref, tau_ref, o_ref,
                       acc_sc, pa_sc, pb_sc, *, nsteps):
    j = pl.program_id(1)
    gate_refs = (s1_ref, e1_ref, s2_ref, e2_ref, tau_ref)

    @pl.when(j == 0)
    def _():
        acc_sc[...] = jnp.zeros(acc_sc.shape, F32)
        pb_sc[...] = jnp.zeros(pb_sc.shape, BF16)

    @pl.when(j < nsteps)
    def _():
        h = h_ref[...]
        act_a = _qk(u_ref[:PEER_TE, :], h)
        acc_sc[...] += jnp.dot(vtp_ref[...], pb_sc[...], preferred_element_type=F32)
        act_b = _qk(u_ref[PEER_TE:, :], h)
        _peer_gates(act_a, 0, *gate_refs, pa_sc)
        acc_sc[...] += jnp.dot(vtc_ref[...], pa_sc[...], preferred_element_type=F32)
        _peer_gates(act_b, PEER_TE // PEER_N_KEYS, *gate_refs, pb_sc)

    @pl.when(j == nsteps)
    def _():
        o_ref[...] = (acc_sc[...] + jnp.dot(vtp_ref[...], pb_sc[...], preferred_element_type=F32)).T


def peer_dense(h, u, vt, s1, s2, e1, e2, tau):
    M, D = h.shape
    E = u.shape[0]
    tm = _pick(M, (512, 384, 256, 128))
    te = PEER_TE
    ns = E // (2 * te)
    H, K = PEER_HEADS, PEER_N_KEYS
    kstep = PEER_K1_PER_STEP
    s1b = s1.reshape(H, K // kstep, kstep, M)
    e1b = e1.reshape(H, K // kstep, kstep, M)
    big = pl.BlockSpec((H, K, tm), lambda i, j: (0, 0, i))
    small = pl.BlockSpec((H, None, kstep, tm), lambda i, j: (0, jnp.minimum(j, ns - 1), 0, i))
    return pl.pallas_call(
        functools.partial(_peer_dense_kernel, nsteps=ns),
        grid=(M // tm, ns + 1),
        in_specs=[pl.BlockSpec((tm, D), lambda i, j: (i, 0)),
                  pl.BlockSpec((2 * te, D), lambda i, j: (jnp.minimum(j, ns - 1), 0)),
                  pl.BlockSpec((D, te), lambda i, j: (0, jnp.maximum(2 * j - 1, 0))),
                  pl.BlockSpec((D, te), lambda i, j: (0, jnp.minimum(2 * j, 2 * ns - 1))),
                  small, small, big, big,
                  pl.BlockSpec((H, 1, tm), lambda i, j: (0, 0, i))],
        out_specs=pl.BlockSpec((tm, D), lambda i, j: (i, 0)),
        out_shape=jax.ShapeDtypeStruct((M, D), F32),
        scratch_shapes=[pltpu.VMEM((D, tm), F32), pltpu.VMEM((te, tm), BF16), pltpu.VMEM((te, tm), BF16)],
        compiler_params=_cparams(("parallel", "arbitrary")),
        name="peer_dense",
    )(h, u, vt, vt, s1b, e1b, s2, e2, tau)


def peer_ffn(h, w_q, k1, k2, u, vt):
    q = matmul(h, w_q, BF16, "peer_q")
    s1, s2, e1, e2, tau = peer_route(q, k1, k2)
    return peer_dense(h, u, vt, s1, s2, e1, e2, tau)


def _mla_weights(w_a, w_uq, w_ukv):
    H, dn, dr, dv = MLA_HEADS, MLA_NOPE, MLA_ROPE, MLA_V
    pad_a = jnp.zeros((w_a.shape[0], LANES - dr), w_a.dtype)
    w_a_p = jnp.concatenate([w_a, pad_a], axis=1).astype(BF16)
    wq = w_uq.reshape(MLA_Q_RANK, H, dn + dr)
    wq_n = wq[:, :, :dn].reshape(MLA_Q_RANK, H * dn)
    wq_r = jnp.concatenate([wq[:, :, dn:], jnp.zeros((MLA_Q_RANK, H, LANES - dr), wq.dtype)], axis=2)
    w_q_p = jnp.concatenate([wq_n, wq_r.reshape(MLA_Q_RANK, H * LANES)], axis=1).astype(BF16)
    wkv = w_ukv.reshape(MLA_KV_RANK, H, dn + dv)
    w_kv_p = jnp.concatenate([wkv[:, :, :dn].reshape(MLA_KV_RANK, H * dn),
                              wkv[:, :, dn:].reshape(MLA_KV_RANK, H * dv)], axis=1).astype(BF16)
    return w_a_p, w_q_p, w_kv_p


def mla_mixer(h, w_a, q_g, kv_g, w_uq, w_ukv, w_o, tables, n_lat, n_ctx):
    B, S, D = h.shape
    M = B * S
    w_a_p, w_q_p, w_kv_p = _mla_weights(w_a, w_uq, w_ukv)
    ca = matmul(h.reshape(M, D), w_a_p, F32, "mla_a").reshape(B, S, -1)
    qc, kvc, krp = mla_prep(ca, q_g, kv_g, tables)
    q = matmul(qc.reshape(M, -1), w_q_p, BF16, "mla_q").reshape(B, S, -1)
    kv = matmul(kvc.reshape(M, -1), w_kv_p, BF16, "mla_kv").reshape(B, S, -1)
    o = mla_attention_all(q, kv, krp, tables, n_lat, n_ctx)
    return matmul(o.reshape(M, -1), w_o.astype(BF16), F32, "mla_o").reshape(B, S, D)


def ssd_mixer(h, w_in, conv_w, conv_b, a_log, dt_bias, d_skip, norm_g, w_out, n_lat):
    B, S, D = h.shape
    M = B * S
    G, HG = SSM_GROUPS, SSM_HG
    h2 = h.reshape(M, D)
    w_bf = w_in.astype(BF16)
    z = matmul(h2, w_bf[:, :SSM_D_INNER], BF16, "ssd_in_z").reshape(B, S, -1)
    xbc = matmul(h2, w_bf[:, SSM_D_INNER:SSM_D_INNER + SSM_CONV_DIM], BF16, "ssd_in_x").reshape(B, S, -1)
    dt_raw = matmul(h2, w_bf[:, SSM_D_INNER + SSM_CONV_DIM:], F32, "ssd_in_dt").reshape(B, S, -1)
    xbc_act, dt = ssd_conv(xbc, dt_raw, conv_w, conv_b, dt_bias.reshape(-1), n_lat)
    dt5 = dt.reshape(B, S, 2, G, HG).transpose(0, 2, 3, 1, 4)
    dtt5 = dt5.transpose(0, 1, 2, 4, 3)
    a = -jnp.exp(a_log.astype(F32)).reshape(2, G, HG)
    xt = jnp.swapaxes(xbc_act[:, :, :SSM_D_INNER], 1, 2)
    y2 = ssd_scan(xbc_act, xt, dt5, dtt5, a[:, :, None, :], a[:, :, :, None], n_lat)
    d_full = jnp.repeat(d_skip.astype(F32), SSM_HEADDIM)
    yn = ssd_finish(y2, xbc_act, z, d_full, norm_g)
    return matmul(yn.reshape(M, -1), w_out.astype(BF16), F32, "ssd_out").reshape(B, S, D)


def window_mixer(h, w_qkv, sinks, w_o, tables, n_lat, n_ctx):
    B, S, D = h.shape
    M = B * S
    qkv = matmul(h.reshape(M, D), w_qkv.astype(BF16), BF16, "win_qkv").reshape(B, S, -1)
    qk = rope_cols(qkv, (WQ_HEADS + WKV_HEADS) * W_HEAD_DIM, tables)
    o = window_attention(qk, qkv, sinks.astype(F32), n_lat, n_ctx)
    return matmul(o.reshape(M, -1), w_o.astype(BF16), F32, "win_o").reshape(B, S, D)


def kernel(x, c, ctx, c_ctx, ada_w, ada_b, norm_mix_g, norm_ffn_g, mla_w_a, mla_q_norm_g, mla_kv_norm_g,
           mla_w_uq, mla_w_ukv, mla_w_o, ssd_w_in, ssd_conv_w, ssd_conv_b, ssd_a_log, ssd_dt_bias,
           ssd_d_skip, ssd_norm_g, ssd_w_out, win_w_qkv, win_sinks, win_w_o, peer_w_q, peer_k1, peer_k2,
           peer_u, peer_v, final_norm_g):
    B, L, D = x.shape
    Lc = ctx.shape[1]
    S = L + Lc
    M = B * S
    depth = ada_w.shape[0]
    xs = jnp.concatenate([x, ctx], axis=1)

    n_grp = 8
    cmat = jnp.zeros((n_grp, D), F32).at[:B].set(c).at[B].set(c_ctx)
    mod_all = adaln_all(cmat, ada_w, ada_b).reshape(depth, n_grp, ADA_CHUNKS, 1, D)

    tab_mla = rope_tables(L, Lc, repeat_upper=False)
    tab_win = rope_tables(L, Lc, repeat_upper=True)

    h = None
    ffn = None
    for i in range(depth):
        mod = mod_all[i]
        if i == 0:
            (h,) = resid_norm_mod(xs, None, None, None, norm_mix_g[i], mod, 0, 1, L)
        else:
            xs, h = resid_norm_mod(xs, ffn, mod_all[i - 1], 5, norm_mix_g[i], mod, 0, 1, L)
        kind, j = i % N_MIXERS, i // N_MIXERS
        if kind == 0:
            mix = mla_mixer(h, mla_w_a[j], mla_q_norm_g[j], mla_kv_norm_g[j], mla_w_uq[j], mla_w_ukv[j],
                            mla_w_o[j], tab_mla, L, Lc)
        elif kind == 1:
            mix = ssd_mixer(h, ssd_w_in[j], ssd_conv_w[j], ssd_conv_b[j], ssd_a_log[j], ssd_dt_bias[j],
                            ssd_d_skip[j], ssd_norm_g[j], ssd_w_out[j], L)
        else:
            mix = window_mixer(h, win_w_qkv[j], win_sinks[j], win_w_o[j], tab_win, L, Lc)
        xs, h2 = resid_norm_mod(xs, mix, mod, 2, norm_ffn_g[i], mod, 3, 4, L)
        rows = L if i == depth - 1 else S
        ffn = peer_ffn(h2[:, :rows].reshape(B * rows, D), peer_w_q[i].astype(BF16),
                       peer_k1[i].astype(BF16), peer_k2[i].astype(BF16),
                       peer_u[i].astype(BF16), peer_v[i].T.astype(BF16)).reshape(B, rows, D)
    (out,) = resid_norm_mod(xs, ffn, mod_all[depth - 1], 5, final_norm_g, None, None, None, L, final=True)
    return out
```
